```python
import math
import jax
import jax.numpy as jnp
from jax import lax
import numpy as np

D_MODEL = 4096
BATCH = 1
SEQ = 16384
DEPTH = 2

LRU_WIDTH = 1536
LRU_BLOCKS = 12
LRU_BLOCK_DIM = LRU_WIDTH // LRU_BLOCKS
LRU_CONV = 4
LRU_C = 8.0
ATT_HEADS = 12
HEAD_DIM = 128
ATT_WIDTH = ATT_HEADS * HEAD_DIM
MOBA_BLOCK = 256
MOBA_TOPK = 3
Q_CHUNK = 64
SC_WIDTH = 1024
SC_CONV = 3
REL_BUCKETS = 32
REL_MAX_DIST = 2048
FF_RAW = -(-8 * D_MODEL // 3)
D_FF = 256 * (-(-FF_RAW // 256))
N_BRANCH = 3
MIX_WIDTH = LRU_WIDTH + ATT_WIDTH + SC_WIDTH
IN_WIDTH = 2 * LRU_WIDTH + 3 * ATT_WIDTH + 3 * SC_WIDTH
EPS = 1e-6
NEG = -1e30

kernel_name = 'hybrid_rglru_moba_shortconv'


def rms_norm(x, g):
    xf = x.astype(jnp.float32)
    y = xf * lax.rsqrt(jnp.mean(xf * xf, axis=-1, keepdims=True) + EPS)
    return (y * g.astype(jnp.float32)).astype(x.dtype)


def causal_depthwise_conv(x, w):
    width, ch = w.shape
    return lax.conv_general_dilated(
        x, w[:, None, :].astype(x.dtype), window_strides=(1,),
        padding=[(width - 1, 0)], dimension_numbers=('NWC', 'WIO', 'NWC'),
        feature_group_count=ch)


def t5_bucket(dist):
    n = jnp.maximum(dist, 0)
    max_exact = REL_BUCKETS // 2
    nf = jnp.maximum(n, 1).astype(jnp.float32)
    large = max_exact + (jnp.log(nf / max_exact) / math.log(REL_MAX_DIST / max_exact)
                         * (REL_BUCKETS - max_exact)).astype(jnp.int32)
    large = jnp.minimum(large, REL_BUCKETS - 1)
    return jnp.where(n < max_exact, n, large)


def rg_lru(x, w_a, b_a, w_x, b_x, lam):
    bsz, seq, _ = x.shape
    xb = x.reshape(bsz, seq, LRU_BLOCKS, LRU_BLOCK_DIM)
    r = jax.nn.sigmoid((jnp.einsum('bsgi,gio->bsgo', xb, w_a).reshape(bsz, seq, LRU_WIDTH) + b_a).astype(jnp.float32))
    i = jax.nn.sigmoid((jnp.einsum('bsgi,gio->bsgo', xb, w_x).reshape(bsz, seq, LRU_WIDTH) + b_x).astype(jnp.float32))
    log_a = -LRU_C * r * jax.nn.softplus(-lam.astype(jnp.float32))
    a = jnp.exp(log_a)
    u = jnp.sqrt(-jnp.expm1(2.0 * log_a)) * (i * x.astype(jnp.float32))

    def combine(left, right):
        a1, b1 = left
        a2, b2 = right
        return a1 * a2, a2 * b1 + b2

    _, h = lax.associative_scan(combine, (a, u), axis=1)
    return h.astype(x.dtype)


def moba_attention(q, k, v, rel_table):
    bsz, n_heads, seq, dh = q.shape
    nb = -(-seq // MOBA_BLOCK)
    s_pad = nb * MOBA_BLOCK
    pad = ((0, 0), (0, 0), (0, s_pad - seq), (0, 0))
    kp = jnp.pad(k, pad)
    vp = jnp.pad(v, pad)
    kb = kp.reshape(bsz, n_heads, nb, MOBA_BLOCK, dh)
    vb = vp.reshape(bsz, n_heads, nb, MOBA_BLOCK, dh)
    k_mean = jnp.mean(kb.astype(jnp.float32), axis=3)
    topk = min(MOBA_TOPK, nb)
    n_chunks = seq // Q_CHUNK
    q_chunks = jnp.moveaxis(q.reshape(bsz, n_heads, n_chunks, Q_CHUNK, dh), 2, 0)
    scale = HEAD_DIM ** -0.5
    table = rel_table.astype(jnp.float32)
    table_h = table.T
    h_ix = jnp.arange(n_heads)[None, :, None, None, None]
    gather_blocks = jax.vmap(jax.vmap(lambda blocks, idx: blocks[idx]))

    def chunk(args):
        qc, c = args
        q0 = c * Q_CHUNK
        q_pos = q0 + jnp.arange(Q_CHUNK)
        own = q0 // MOBA_BLOCK
        gate = jnp.einsum('bhqd,bhnd->bhqn', qc.astype(jnp.float32), k_mean)
        gate = jnp.where(jnp.arange(nb) < own, gate, NEG)
        _, idx = lax.top_k(gate, topk)
        sel_valid = idx < own
        k_sel = gather_blocks(kb, idx)
        v_sel = gather_blocks(vb, idx)
        s_sel = jnp.einsum('bhqd,bhqnkd->bhqnk', qc, k_sel).astype(jnp.float32) * scale
        k_pos_sel = idx[..., None] * MOBA_BLOCK + jnp.arange(MOBA_BLOCK)
        s_sel = s_sel + table_h[h_ix, t5_bucket(q_pos[None, None, :, None, None] - k_pos_sel)]
        s_sel = jnp.where(sel_valid[..., None], s_sel, NEG)
        k_own = lax.dynamic_slice_in_dim(kp, own * MOBA_BLOCK, MOBA_BLOCK, axis=2)
        v_own = lax.dynamic_slice_in_dim(vp, own * MOBA_BLOCK, MOBA_BLOCK, axis=2)
        rel = q_pos[:, None] - (own * MOBA_BLOCK + jnp.arange(MOBA_BLOCK))[None, :]
        s_own = jnp.einsum('bhqd,bhkd->bhqk', qc, k_own).astype(jnp.float32) * scale
        s_own = s_own + jnp.moveaxis(table[t5_bucket(rel)], -1, 0)
        s_own = jnp.where(rel >= 0, s_own, NEG)
        logits = jnp.concatenate([s_sel.reshape(bsz, n_heads, Q_CHUNK, topk * MOBA_BLOCK), s_own], axis=-1)
        p = jax.nn.softmax(logits, axis=-1)
        p_sel = p[..., :topk * MOBA_BLOCK].reshape(bsz, n_heads, Q_CHUNK, topk, MOBA_BLOCK).astype(v.dtype)
        p_own = p[..., topk * MOBA_BLOCK:].astype(v.dtype)
        return (jnp.einsum('bhqnk,bhqnkd->bhqd', p_sel, v_sel)
                + jnp.einsum('bhqk,bhkd->bhqd', p_own, v_own))

    out = lax.map(chunk, (q_chunks, jnp.arange(n_chunks, dtype=jnp.int32)))
    return jnp.moveaxis(out, 0, 2).reshape(bsz, n_heads, seq, dh)


def hybrid_layer(x, rel_bias, g_pre_mix, g_post_mix, g_pre_ffn, g_post_ffn, w_in,
                 conv_a_w, conv_a_b, lru_wa, lru_ba, lru_wx, lru_bx, lru_lambda,
                 conv_c_w, w_branch, w_gate, w_out, w_ffn_in, w_ffn_out):
    bsz, seq, _ = x.shape
    h = rms_norm(x, g_pre_mix)
    proj = jnp.einsum('bsd,de->bse', h, w_in)
    s1 = LRU_WIDTH
    s2 = s1 + LRU_WIDTH
    s3 = s2 + ATT_WIDTH
    s4 = s3 + ATT_WIDTH
    s5 = s4 + ATT_WIDTH
    s6 = s5 + SC_WIDTH
    s7 = s6 + SC_WIDTH
    xa, ga, q, k, v, sb, sc, sx = jnp.split(proj, [s1, s2, s3, s4, s5, s6, s7], axis=-1)

    xa = causal_depthwise_conv(xa, conv_a_w) + conv_a_b
    ya = jax.nn.gelu(ga) * rg_lru(xa, lru_wa, lru_ba, lru_wx, lru_bx, lru_lambda)

    def heads(t):
        return t.reshape(bsz, seq, ATT_HEADS, HEAD_DIM).transpose(0, 2, 1, 3)
    yb = moba_attention(heads(q), heads(k), heads(v), rel_bias)
    yb = yb.transpose(0, 2, 1, 3).reshape(bsz, seq, ATT_WIDTH)

    yc = sb * causal_depthwise_conv(sc * sx, conv_c_w)

    pa = jnp.einsum('bsi,id->bsd', ya, w_branch[:LRU_WIDTH])
    pb = jnp.einsum('bsi,id->bsd', yb, w_branch[LRU_WIDTH:LRU_WIDTH + ATT_WIDTH])
    pc = jnp.einsum('bsi,id->bsd', yc, w_branch[LRU_WIDTH + ATT_WIDTH:])
    gates = jax.nn.sigmoid(jnp.einsum('bsd,de->bse', h, w_gate).astype(jnp.float32)).astype(x.dtype)
    gates = gates.reshape(bsz, seq, N_BRANCH, D_MODEL)
    merged = gates[:, :, 0] * pa + gates[:, :, 1] * pb + gates[:, :, 2] * pc
    mix = jnp.einsum('bsd,de->bse', merged, w_out)
    x = x + rms_norm(mix, g_post_mix)

    h2 = rms_norm(x, g_pre_ffn)
    gu = jnp.einsum('bsd,df->bsf', h2, w_ffn_in)
    f_gate, f_up = jnp.split(gu, [D_FF], axis=-1)
    f = jnp.einsum('bsf,fd->bsd', jax.nn.silu(f_gate) * f_up, w_ffn_out)
    return x + rms_norm(f, g_post_ffn)


def setup_inputs(seed: int = 0) -> dict:
    key = jax.random.key(seed)
    ks = jax.random.split(key, 24)
    f32 = jnp.float32

    def nrm(k, shape, scale):
        return jax.random.normal(k, shape, f32) * scale

    x = nrm(ks[0], (BATCH, SEQ, D_MODEL), 1.0)
    rel_bias = nrm(ks[1], (REL_BUCKETS, ATT_HEADS), 0.5)
    norm_pre_mix = 1.0 + nrm(ks[2], (DEPTH, D_MODEL), 0.05)
    norm_post_mix = 1.0 + nrm(ks[3], (DEPTH, D_MODEL), 0.05)
    norm_pre_ffn = 1.0 + nrm(ks[4], (DEPTH, D_MODEL), 0.05)
    norm_post_ffn = 1.0 + nrm(ks[5], (DEPTH, D_MODEL), 0.05)
    w_in = nrm(ks[6], (DEPTH, D_MODEL, IN_WIDTH), D_MODEL ** -0.5)
    conv_a_w = nrm(ks[7], (DEPTH, LRU_CONV, LRU_WIDTH), LRU_CONV ** -0.5)
    conv_a_b = nrm(ks[8], (DEPTH, LRU_WIDTH), 0.01)
    lru_wa = nrm(ks[9], (DEPTH, LRU_BLOCKS, LRU_BLOCK_DIM, LRU_BLOCK_DIM), LRU_BLOCK_DIM ** -0.5)
    lru_ba = nrm(ks[10], (DEPTH, LRU_WIDTH), 0.01)
    lru_wx = nrm(ks[11], (DEPTH, LRU_BLOCKS, LRU_BLOCK_DIM, LRU_BLOCK_DIM), LRU_BLOCK_DIM ** -0.5)
    lru_bx = nrm(ks[12], (DEPTH, LRU_WIDTH), 0.01)
    a0 = jax.random.uniform(ks[13], (DEPTH, LRU_WIDTH), f32, 0.9, 0.999)
    lru_lambda = jnp.log(a0) - jnp.log1p(-a0)
    conv_c_w = nrm(ks[14], (DEPTH, SC_CONV, SC_WIDTH), SC_CONV ** -0.5)
    row_scale = jnp.concatenate([
        jnp.full((LRU_WIDTH,), LRU_WIDTH ** -0.5, f32),
        jnp.full((ATT_WIDTH,), ATT_WIDTH ** -0.5, f32),
        jnp.full((SC_WIDTH,), SC_WIDTH ** -0.5, f32)])
    w_branch = nrm(ks[15], (DEPTH, MIX_WIDTH, D_MODEL), 1.0) * row_scale[None, :, None]
    w_gate = nrm(ks[16], (DEPTH, D_MODEL, N_BRANCH * D_MODEL), D_MODEL ** -0.5)
    w_out = nrm(ks[17], (DEPTH, D_MODEL, D_MODEL), D_MODEL ** -0.5)
    w_ffn_in = nrm(ks[18], (DEPTH, D_MODEL, 2 * D_FF), D_MODEL ** -0.5)
    w_ffn_out = nrm(ks[19], (DEPTH, D_FF, D_MODEL), D_FF ** -0.5)
    return {'x': x, 'rel_bias': rel_bias, 'norm_pre_mix': norm_pre_mix,
            'norm_post_mix': norm_post_mix, 'norm_pre_ffn': norm_pre_ffn,
            'norm_post_ffn': norm_post_ffn, 'w_in': w_in, 'conv_a_w': conv_a_w,
            'conv_a_b': conv_a_b, 'lru_wa': lru_wa, 'lru_ba': lru_ba, 'lru_wx': lru_wx,
            'lru_bx': lru_bx, 'lru_lambda': lru_lambda, 'conv_c_w': conv_c_w,
            'w_branch': w_branch, 'w_gate': w_gate, 'w_out': w_out,
            'w_ffn_in': w_ffn_in, 'w_ffn_out': w_ffn_out}


def reference(x, rel_bias, norm_pre_mix, norm_post_mix, norm_pre_ffn, norm_post_ffn,
              w_in, conv_a_w, conv_a_b, lru_wa, lru_ba, lru_wx, lru_bx, lru_lambda,
              conv_c_w, w_branch, w_gate, w_out, w_ffn_in, w_ffn_out):
    for l in range(DEPTH):
        x = hybrid_layer(x, rel_bias, norm_pre_mix[l], norm_post_mix[l], norm_pre_ffn[l],
                         norm_post_ffn[l], w_in[l], conv_a_w[l], conv_a_b[l], lru_wa[l],
                         lru_ba[l], lru_wx[l], lru_bx[l], lru_lambda[l], conv_c_w[l],
                         w_branch[l], w_gate[l], w_out[l], w_ffn_in[l], w_ffn_out[l])
    return x
```

```python
import functools
import math

import numpy as np
import jax
import jax.numpy as jnp
from jax import lax
from jax.experimental import pallas as pl
from jax.experimental.pallas import tpu as pltpu

F32 = jnp.float32
BF16 = jnp.bfloat16

EPS = 1e-6
NEG = -1e30
LOG2E = 1.4426950408889634

HEAD_DIM = 128
MOBA_BLOCK = 256
MOBA_TOPK = 3
REL_BUCKETS = 32
REL_MAX_DIST = 2048
LRU_BLOCK_DIM = 128
LRU_C = 8.0
N_BRANCH = 3
CHAN_TILE = 512
TIME_TILE = 256
CONV_HALO = 8
VMEM_LIMIT = 56 * 1024 * 1024


def _pick(n, candidates):
    for c in candidates:
        if n % c == 0:
            return c
    raise ValueError(f"no tile in {candidates} divides {n}")


def _params(*sem):
    return pltpu.CompilerParams(dimension_semantics=sem, vmem_limit_bytes=VMEM_LIMIT)


def _rms(x, g):
    return x * lax.rsqrt(jnp.mean(x * x, axis=-1, keepdims=True) + EPS) * g


def _norm_kernel(x_ref, g_ref, h_ref):
    h_ref[...] = _rms(x_ref[...], g_ref[...]).astype(h_ref.dtype)


def _resnorm_kernel(x_ref, y_ref, gp_ref, gn_ref, xo_ref, h_ref):
    xn = x_ref[...] + _rms(y_ref[...].astype(F32), gp_ref[...])
    xo_ref[...] = xn
    h_ref[...] = _rms(xn, gn_ref[...]).astype(h_ref.dtype)


def _res_kernel(x_ref, y_ref, gp_ref, xo_ref):
    xo_ref[...] = x_ref[...] + _rms(y_ref[...].astype(F32), gp_ref[...])


def _norm(x, g):
    s, d = x.shape
    bm = _pick(s, (256, 128, 8))
    row = pl.BlockSpec((bm, d), lambda i: (i, 0))
    vec = pl.BlockSpec((1, d), lambda i: (0, 0))
    return pl.pallas_call(
        _norm_kernel, grid=(s // bm,), in_specs=[row, vec], out_specs=row,
        out_shape=jax.ShapeDtypeStruct((s, d), BF16),
        compiler_params=_params("parallel"), name="rmsnorm")(x, g.reshape(1, d))


def _residual_norm(x, y, g_post, g_next):
    s, d = x.shape
    bm = _pick(s, (256, 128, 8))
    row = pl.BlockSpec((bm, d), lambda i: (i, 0))
    vec = pl.BlockSpec((1, d), lambda i: (0, 0))
    if g_next is None:
        return pl.pallas_call(
            _res_kernel, grid=(s // bm,), in_specs=[row, row, vec], out_specs=row,
            out_shape=jax.ShapeDtypeStruct((s, d), F32),
            compiler_params=_params("parallel"), name="residual")(x, y, g_post.reshape(1, d)), None
    return pl.pallas_call(
        _resnorm_kernel, grid=(s // bm,), in_specs=[row, row, vec, vec], out_specs=[row, row],
        out_shape=[jax.ShapeDtypeStruct((s, d), F32), jax.ShapeDtypeStruct((s, d), BF16)],
        compiler_params=_params("parallel"), name="residual_norm")(
            x, y, g_post.reshape(1, d), g_next.reshape(1, d))


def _mm_kernel(a_ref, w_ref, o_ref, *, sigmoid):
    acc = jnp.dot(a_ref[...], w_ref[...], preferred_element_type=F32)
    if sigmoid:
        acc = jax.nn.sigmoid(acc)
    o_ref[...] = acc.astype(o_ref.dtype)


def _matmul(a, w, out_dtype, *, bm, bn, sigmoid=False, name):
    m, k = a.shape
    n = w.shape[1]
    return pl.pallas_call(
        functools.partial(_mm_kernel, sigmoid=sigmoid),
        grid=(m // bm, n // bn),
        in_specs=[pl.BlockSpec((bm, k), lambda i, j: (i, 0)),
                  pl.BlockSpec((k, bn), lambda i, j: (0, j))],
        out_specs=pl.BlockSpec((bm, bn), lambda i, j: (i, j)),
        out_shape=jax.ShapeDtypeStruct((m, n), out_dtype),
        compiler_params=_params("parallel", "arbitrary"), name=name)(a, w)


def _merge_kernel(ya_ref, yb_ref, yc_ref, wa_ref, wb_ref, wc_ref, ga_ref, gb_ref, gc_ref, o_ref):
    pa = jnp.dot(ya_ref[...], wa_ref[...], preferred_element_type=F32)
    pb = jnp.dot(yb_ref[...], wb_ref[...], preferred_element_type=F32)
    pc = jnp.dot(yc_ref[...], wc_ref[...], preferred_element_type=F32)
    merged = (ga_ref[...].astype(F32) * pa + gb_ref[...].astype(F32) * pb
              + gc_ref[...].astype(F32) * pc)
    o_ref[...] = merged.astype(o_ref.dtype)


def _merge(ya, yb, yc, wa, wb, wc, gates, *, bm, bn):
    s = ya.shape[0]
    d = wa.shape[1]
    nj = d // bn

    def act(w):
        return pl.BlockSpec((bm, w), lambda i, j: (i, 0))

    def wgt(k):
        return pl.BlockSpec((k, bn), lambda i, j: (0, j))

    def gate(b):
        return pl.BlockSpec((bm, bn), lambda i, j: (i, b * nj + j))

    return pl.pallas_call(
        _merge_kernel, grid=(s // bm, nj),
        in_specs=[act(ya.shape[1]), act(yb.shape[1]), act(yc.shape[1]),
                  wgt(wa.shape[0]), wgt(wb.shape[0]), wgt(wc.shape[0]),
                  gate(0), gate(1), gate(2)],
        out_specs=pl.BlockSpec((bm, bn), lambda i, j: (i, j)),
        out_shape=jax.ShapeDtypeStruct((s, d), BF16),
        compiler_params=_params("parallel", "arbitrary"), name="branch_merge")(
            ya, yb, yc, wa, wb, wc, gates, gates, gates)


def _ffn_in_kernel(h_ref, wg_ref, wu_ref, o_ref):
    h = h_ref[...]
    g = jnp.dot(h, wg_ref[...], preferred_element_type=F32)
    u = jnp.dot(h, wu_ref[...], preferred_element_type=F32)
    o_ref[...] = (g * jax.nn.sigmoid(g) * u).astype(o_ref.dtype)


def _ffn_in(h, w, *, bm, bn):
    s, d = h.shape
    d_ff = w.shape[1] // 2
    nj = d_ff // bn
    return pl.pallas_call(
        _ffn_in_kernel, grid=(s // bm, nj),
        in_specs=[pl.BlockSpec((bm, d), lambda i, j: (i, 0)),
                  pl.BlockSpec((d, bn), lambda i, j: (0, j)),
                  pl.BlockSpec((d, bn), lambda i, j: (0, nj + j))],
        out_specs=pl.BlockSpec((bm, bn), lambda i, j: (i, j)),
        out_shape=jax.ShapeDtypeStruct((s, d_ff), BF16),
        compiler_params=_params("parallel", "arbitrary"), name="ffn_in")(h, w, w)


def _causal_conv(x, w_ref, xbuf, width):
    t = x.shape[0]
    xbuf[CONV_HALO:CONV_HALO + t, :] = x
    out = None
    for k in range(width):
        off = CONV_HALO - (width - 1) + k
        term = w_ref[k:k + 1, :] * xbuf[off:off + t, :]
        out = term if out is None else out + term
    xbuf[0:CONV_HALO, :] = xbuf[t:t + CONV_HALO, :]
    return out


def _linear_scan(a, u):
    t = a.shape[0]
    row = lax.broadcasted_iota(jnp.int32, a.shape, 0)
    s = 1
    while s < t:
        keep = row >= s
        a_prev = pltpu.roll(a, s, 0)
        u_prev = pltpu.roll(u, s, 0)
        u = jnp.where(keep, a * u_prev + u, u)
        a = jnp.where(keep, a * a_prev, a)
        s *= 2
    return a, u


def _mix_a_kernel(xa_ref, ga_ref, cw_ref, cb_ref, wa_ref, ba_ref, wx_ref, bx_ref, lam_ref,
                  o_ref, xbuf, hcar):
    @pl.when(pl.program_id(1) == 0)
    def _():
        xbuf[0:CONV_HALO, :] = jnp.zeros((CONV_HALO, xbuf.shape[1]), F32)
        hcar[...] = jnp.zeros_like(hcar)

    t, c = xa_ref.shape
    x = _causal_conv(xa_ref[...].astype(F32), cw_ref, xbuf, cw_ref.shape[0]) + cb_ref[...]
    xb = x.astype(BF16)
    r_parts, i_parts = [], []
    for g in range(c // LRU_BLOCK_DIM):
        xg = xb[:, g * LRU_BLOCK_DIM:(g + 1) * LRU_BLOCK_DIM]
        r_parts.append(jnp.dot(xg, wa_ref[g], preferred_element_type=F32))
        i_parts.append(jnp.dot(xg, wx_ref[g], preferred_element_type=F32))
    r = jax.nn.sigmoid(jnp.concatenate(r_parts, axis=1) + ba_ref[...])
    i = jax.nn.sigmoid(jnp.concatenate(i_parts, axis=1) + bx_ref[...])
    z = -lam_ref[...]
    softplus = jnp.maximum(z, 0.0) + jnp.log1p(jnp.exp(-jnp.abs(z)))
    log_a = (-LRU_C) * r * softplus
    a = jnp.exp(log_a)
    th = jnp.tanh(log_a)
    u = jnp.sqrt(-2.0 * th / (1.0 - th)) * (i * x)
    a_cum, h = _linear_scan(a, u)
    h = h + a_cum * hcar[...]
    hcar[...] = h[t - 1:t, :]
    o_ref[...] = (jax.nn.gelu(ga_ref[...].astype(F32)) * h).astype(o_ref.dtype)


def _mix_a(proj, col_x, col_g, conv_w, conv_b, w_a, b_a, w_x, b_x, lam):
    s = proj.shape[0]
    width = conv_w.shape[1]
    tc = CHAN_TILE
    tt = _pick(s, (TIME_TILE,))
    gpt = tc // LRU_BLOCK_DIM
    x_off, g_off = col_x // tc, col_g // tc
    vec = pl.BlockSpec((1, tc), lambda c, t: (0, c))
    blk = pl.BlockSpec((gpt, LRU_BLOCK_DIM, LRU_BLOCK_DIM), lambda c, t: (c, 0, 0))
    return pl.pallas_call(
        _mix_a_kernel, grid=(width // tc, s // tt),
        in_specs=[pl.BlockSpec((tt, tc), lambda c, t: (t, x_off + c)),
                  pl.BlockSpec((tt, tc), lambda c, t: (t, g_off + c)),
                  pl.BlockSpec((conv_w.shape[0], tc), lambda c, t: (0, c)),
                  vec, blk, vec, blk, vec, vec],
        out_specs=pl.BlockSpec((tt, tc), lambda c, t: (t, c)),
        out_shape=jax.ShapeDtypeStruct((s, width), BF16),
        scratch_shapes=[pltpu.VMEM((tt + CONV_HALO, tc), F32), pltpu.VMEM((1, tc), F32)],
        compiler_params=_params("arbitrary", "arbitrary"), name="mixer_rglru")(
            proj, proj, conv_w, conv_b.reshape(1, width), w_a.astype(BF16), b_a.reshape(1, width),
            w_x.astype(BF16), b_x.reshape(1, width), lam.reshape(1, width))


def _mix_c_kernel(sb_ref, sc_ref, sx_ref, cw_ref, o_ref, xbuf):
    @pl.when(pl.program_id(1) == 0)
    def _():
        xbuf[0:CONV_HALO, :] = jnp.zeros((CONV_HALO, xbuf.shape[1]), F32)

    z = sc_ref[...].astype(F32) * sx_ref[...].astype(F32)
    conv = _causal_conv(z, cw_ref, xbuf, cw_ref.shape[0])
    o_ref[...] = (sb_ref[...].astype(F32) * conv).astype(o_ref.dtype)


def _mix_c(proj, col_b, col_c, col_x, conv_w):
    s = proj.shape[0]
    width = conv_w.shape[1]
    tc = CHAN_TILE
    tt = _pick(s, (TIME_TILE,))

    def col(off):
        return pl.BlockSpec((tt, tc), lambda c, t: (t, off // tc + c))

    return pl.pallas_call(
        _mix_c_kernel, grid=(width // tc, s // tt),
        in_specs=[col(col_b), col(col_c), col(col_x),
                  pl.BlockSpec((conv_w.shape[0], tc), lambda c, t: (0, c))],
        out_specs=pl.BlockSpec((tt, tc), lambda c, t: (t, c)),
        out_shape=jax.ShapeDtypeStruct((s, width), BF16),
        scratch_shapes=[pltpu.VMEM((tt + CONV_HALO, tc), F32)],
        compiler_params=_params("arbitrary", "arbitrary"), name="mixer_shortconv")(
            proj, proj, proj, conv_w)


def _t5_bucket_np(dist):
    n = np.maximum(dist, 0)
    max_exact = REL_BUCKETS // 2
    nf = np.maximum(n, 1).astype(np.float32)
    large = max_exact + (np.log(nf / np.float32(max_exact)) / np.float32(math.log(REL_MAX_DIST / max_exact))
                         * np.float32(REL_BUCKETS - max_exact)).astype(np.int32)
    large = np.minimum(large, REL_BUCKETS - 1)
    return np.where(n < max_exact, n, large).astype(np.int32)


def _near_offsets():
    d = 0
    while True:
        lo = d * MOBA_BLOCK - (MOBA_BLOCK - 1)
        if lo > 0 and _t5_bucket_np(np.array([lo]))[0] == REL_BUCKETS - 1:
            return d
        d += 1


N_NEAR = _near_offsets()


def _bucket_tiles():
    kr = np.arange(MOBA_BLOCK)[:, None]
    qc = np.arange(MOBA_BLOCK)[None, :]
    tiles = []
    for d in range(N_NEAR):
        rel = d * MOBA_BLOCK + qc - kr
        tiles.append(np.where(rel >= 0, _t5_bucket_np(rel), REL_BUCKETS))
    return np.stack(tiles).astype(np.int32)


def _bias_kernel(tbl_ref, bucket_ref, o_ref):
    h = pl.program_id(0)
    bucket = bucket_ref[0]
    acc = jnp.full(bucket.shape, NEG, F32)
    for b in range(REL_BUCKETS):
        acc = jnp.where(bucket == b, tbl_ref[h * REL_BUCKETS + b], acc)
    o_ref[0, 0] = acc


def _bias_tiles(tbl, n_heads):
    buckets = jnp.asarray(_bucket_tiles())
    blk = MOBA_BLOCK
    return pl.pallas_call(
        _bias_kernel, grid=(n_heads, N_NEAR),
        in_specs=[pl.BlockSpec(memory_space=pltpu.SMEM),
                  pl.BlockSpec((1, blk, blk), lambda h, d: (d, 0, 0))],
        out_specs=pl.BlockSpec((1, 1, blk, blk), lambda h, d: (h, d, 0, 0)),
        out_shape=jax.ShapeDtypeStruct((n_heads, N_NEAR, blk, blk), F32),
        compiler_params=_params("arbitrary", "arbitrary"), name="t5_bias_tiles")(tbl, buckets)


def _kmean_kernel(k_ref, o_ref):
    k = k_ref[...].astype(F32)
    nb = k.shape[0] // MOBA_BLOCK
    o_ref[0] = jnp.mean(k.reshape(nb, MOBA_BLOCK, k.shape[1]), axis=1)


def _kmean(proj, col_k, n_heads):
    s = proj.shape[0]
    nb = s // MOBA_BLOCK
    off = col_k // HEAD_DIM
    return pl.pallas_call(
        _kmean_kernel, grid=(n_heads,),
        in_specs=[pl.BlockSpec((s, HEAD_DIM), lambda h: (0, off + h))],
        out_specs=pl.BlockSpec((1, nb, HEAD_DIM), lambda h: (h, 0, 0)),
        out_shape=jax.ShapeDtypeStruct((n_heads, nb, HEAD_DIM), F32),
        compiler_params=_params("parallel"), name="moba_kmean")(proj)


_NT = (((1,), (1,)), ((), ()))


def _attn_kernel(tbl_ref, q_ref, k_ref, vt_ref, km_ref, bias_ref, o_ref, neg_ref, acc_ref, *, scale):
    h = pl.program_id(0)
    i = pl.program_id(1)
    blk = MOBA_BLOCK
    nb = neg_ref.shape[0]
    c = scale * LOG2E
    q = q_ref[...]

    km = km_ref[0]
    km_hi = km.astype(BF16)
    km_lo = (km - km_hi.astype(F32)).astype(BF16)
    gate = (lax.dot_general(km_hi, q, _NT, preferred_element_type=F32)
            + lax.dot_general(km_lo, q, _NT, preferred_element_type=F32))
    bidx = lax.broadcasted_iota(jnp.int32, (nb, blk), 0).astype(F32)
    past = bidx < i.astype(F32)
    g = jnp.where(past, gate, NEG)
    chosen = jnp.zeros((nb, blk), jnp.bool_)
    for _ in range(MOBA_TOPK):
        mx = jnp.max(g, axis=0, keepdims=True)
        first = jnp.min(jnp.where(g == mx, bidx, float(nb)), axis=0, keepdims=True)
        pick = bidx == first
        chosen = jnp.logical_or(chosen, pick)
        g = jnp.where(pick, -jnp.inf, g)
    neg_ref[...] = jnp.where(jnp.logical_and(chosen, past), 0.0, NEG)

    def scores(j, bias):
        kb = k_ref[pl.ds(pl.multiple_of(j * blk, blk), blk), :]
        return lax.dot_general(kb, q, _NT, preferred_element_type=F32) + bias

    def weighted_values(j, p):
        return jnp.dot(vt_ref[0, j], p.astype(BF16), preferred_element_type=F32)

    s = scores(i, bias_ref[0, 0])
    m = jnp.max(s, axis=0, keepdims=True)
    p = jnp.exp2((s - m) * c)
    l = jnp.sum(p, axis=0, keepdims=True)
    acc_ref[...] = weighted_values(i, p)

    def step(j, bias, carry):
        m, l = carry
        s = scores(j, bias) + neg_ref[pl.ds(j, 1), :]
        m_new = jnp.maximum(m, jnp.max(s, axis=0, keepdims=True))
        p = jnp.exp2((s - m_new) * c)
        alpha = jnp.exp2((m - m_new) * c)
        acc_ref[...] = acc_ref[...] * alpha + weighted_values(j, p)
        return m_new, alpha * l + jnp.sum(p, axis=0, keepdims=True)

    n_near = jnp.minimum(i, N_NEAR - 1)
    m, l = lax.fori_loop(1, n_near + 1, lambda d, carry: step(i - d, bias_ref[0, d], carry), (m, l))
    far_bias = tbl_ref[h * REL_BUCKETS + REL_BUCKETS - 1]
    m, l = lax.fori_loop(0, i - n_near, lambda j, carry: step(j, far_bias, carry), (m, l))

    o_ref[...] = (acc_ref[...] * (1.0 / l)).T.astype(o_ref.dtype)


def _moba(proj, col_q, col_k, col_v, n_heads, rel_bias):
    s = proj.shape[0]
    blk, dh = MOBA_BLOCK, HEAD_DIM
    nb = s // blk
    scale = dh ** -0.5
    tbl = (rel_bias.astype(F32).T / scale).reshape(-1)
    bias = _bias_tiles(tbl, n_heads)
    km = _kmean(proj, col_k, n_heads)
    v = proj[:, col_v:col_v + n_heads * dh]
    vt = v.reshape(nb, blk, n_heads, dh).transpose(2, 0, 3, 1)
    q_off, k_off = col_q // dh, col_k // dh
    return pl.pallas_call(
        functools.partial(_attn_kernel, scale=scale), grid=(n_heads, nb),
        in_specs=[pl.BlockSpec(memory_space=pltpu.SMEM),
                  pl.BlockSpec((blk, dh), lambda h, i: (i, q_off + h)),
                  pl.BlockSpec((s, dh), lambda h, i: (0, k_off + h)),
                  pl.BlockSpec((1, nb, dh, blk), lambda h, i: (h, 0, 0, 0)),
                  pl.BlockSpec((1, nb, dh), lambda h, i: (h, 0, 0)),
                  pl.BlockSpec((1, N_NEAR, blk, blk), lambda h, i: (h, 0, 0, 0))],
        out_specs=pl.BlockSpec((blk, dh), lambda h, i: (i, h)),
        out_shape=jax.ShapeDtypeStruct((s, n_heads * dh), BF16),
        scratch_shapes=[pltpu.VMEM((nb, blk), F32), pltpu.VMEM((dh, blk), F32)],
        compiler_params=_params("parallel", "arbitrary"), name="moba_attention")(
            tbl, proj, proj, vt, km, bias)


def _layer(x, h, rel_bias, g_post_mix, g_pre_ffn, g_post_ffn, g_next, w_in, conv_a_w, conv_a_b,
           lru_wa, lru_ba, lru_wx, lru_bx, lru_lambda, conv_c_w, w_branch, w_gate, w_out,
           w_ffn_in, w_ffn_out):
    s, d = x.shape
    lru_w = conv_a_w.shape[1]
    sc_w = conv_c_w.shape[1]
    att_w = (w_in.shape[1] - 2 * lru_w - 3 * sc_w) // 3
    n_heads = att_w // HEAD_DIM
    col_ga = lru_w
    col_q = 2 * lru_w
    col_k = col_q + att_w
    col_v = col_k + att_w
    col_sb = col_v + att_w
    col_sc = col_sb + sc_w
    col_sx = col_sc + sc_w

    bm = _pick(s, (1024, 512, 256))
    proj = _matmul(h, w_in.astype(BF16), BF16, bm=bm, bn=_pick(w_in.shape[1], (768, 512, 256, 128)),
                   name="in_proj")
    gates = _matmul(h, w_gate.astype(BF16), BF16, bm=bm, bn=_pick(w_gate.shape[1], (768, 512, 256, 128)),
                    sigmoid=True, name="gate_proj")

    ya = _mix_a(proj, 0, col_ga, conv_a_w, conv_a_b, lru_wa, lru_ba, lru_wx, lru_bx, lru_lambda)
    yb = _moba(proj, col_q, col_k, col_v, n_heads, rel_bias)
    yc = _mix_c(proj, col_sb, col_sc, col_sx, conv_c_w)

    wb = w_branch.astype(BF16)
    merged = _merge(ya, yb, yc, wb[:lru_w], wb[lru_w:lru_w + att_w], wb[lru_w + att_w:], gates,
                    bm=bm, bn=_pick(d, (512, 256, 128)))
    mix = _matmul(merged, w_out.astype(BF16), F32, bm=bm, bn=_pick(d, (512, 256, 128)), name="out_proj")
    x, h2 = _residual_norm(x, mix, g_post_mix, g_pre_ffn)

    act = _ffn_in(h2, w_ffn_in.astype(BF16), bm=bm, bn=_pick(w_ffn_in.shape[1] // 2, (256, 128)))
    f = _matmul(act, w_ffn_out.astype(BF16), F32, bm=_pick(s, (512, 256)), bn=_pick(d, (512, 256, 128)),
                name="ffn_out")
    return _residual_norm(x, f, g_post_ffn, g_next)


@jax.jit
def _forward(x, rel_bias, norm_pre_mix, norm_post_mix, norm_pre_ffn, norm_post_ffn, w_in, conv_a_w,
             conv_a_b, lru_wa, lru_ba, lru_wx, lru_bx, lru_lambda, conv_c_w, w_branch, w_gate, w_out,
             w_ffn_in, w_ffn_out):
    bsz, s, d = x.shape
    depth = w_in.shape[0]
    outs = []
    for b in range(bsz):
        xb = x[b]
        h = _norm(xb, norm_pre_mix[0])
        for l in range(depth):
            g_next = norm_pre_mix[l + 1] if l + 1 < depth else None
            xb, h = _layer(xb, h, rel_bias, norm_post_mix[l], norm_pre_ffn[l], norm_post_ffn[l], g_next,
                           w_in[l], conv_a_w[l], conv_a_b[l], lru_wa[l], lru_ba[l], lru_wx[l], lru_bx[l],
                           lru_lambda[l], conv_c_w[l], w_branch[l], w_gate[l], w_out[l], w_ffn_in[l],
                           w_ffn_out[l])
        outs.append(xb)
    return jnp.stack(outs)


def kernel(x, rel_bias, norm_pre_mix, norm_post_mix, norm_pre_ffn, norm_post_ffn, w_in, conv_a_w, conv_a_b, lru_wa, lru_ba, lru_wx, lru_bx, lru_lambda, conv_c_w, w_branch, w_gate, w_out, w_ffn_in, w_ffn_out):
    return _forward(x, rel_bias, norm_pre_mix, norm_post_mix, norm_pre_ffn, norm_post_ffn, w_in, conv_a_w,
                    conv_a_b, lru_wa, lru_ba, lru_wx, lru_bx, lru_lambda, conv_c_w, w_branch, w_gate, w_out,
                    w_ffn_in, w_ffn_out)
```

```python
import functools
import math

import numpy as np
import jax
import jax.numpy as jnp
from jax import lax
from jax.experimental import pallas as pl
from jax.experimental.pallas import tpu as pltpu

F32 = jnp.float32
BF16 = jnp.bfloat16

EPS = 1e-6
NEG = -1e30
LOG2E = 1.4426950408889634

HEAD_DIM = 128
QK_SCALE = HEAD_DIM ** -0.5 * LOG2E
MOBA_BLOCK = 256
MOBA_TOPK = 3
REL_BUCKETS = 32
REL_MAX_DIST = 2048
LRU_BLOCK_DIM = 128
LRU_C = 8.0
N_BRANCH = 3
CHAN_TILE = 512
TIME_TILE = 256
CONV_HALO = 8
BF16_SUBLANES = 16
VMEM_LIMIT = 56 * 1024 * 1024


def _pick(n, candidates):
    for c in candidates:
        if n % c == 0:
            return c
    raise ValueError(f"no tile in {candidates} divides {n}")


def _params(*sem):
    return pltpu.CompilerParams(dimension_semantics=sem, vmem_limit_bytes=VMEM_LIMIT)


def _rms(x, g):
    return x * lax.rsqrt(jnp.mean(x * x, axis=-1, keepdims=True) + EPS) * g


def _norm_kernel(x_ref, g_ref, h_ref):
    h_ref[...] = _rms(x_ref[...], g_ref[...]).astype(h_ref.dtype)


def _resnorm_kernel(x_ref, y_ref, gp_ref, gn_ref, xo_ref, h_ref):
    xn = x_ref[...] + _rms(y_ref[...].astype(F32), gp_ref[...])
    xo_ref[...] = xn
    h_ref[...] = _rms(xn, gn_ref[...]).astype(h_ref.dtype)


def _res_kernel(x_ref, y_ref, gp_ref, xo_ref):
    xo_ref[...] = x_ref[...] + _rms(y_ref[...].astype(F32), gp_ref[...])


def _norm(x, g):
    s, d = x.shape
    bm = _pick(s, (256, 128, 8))
    row = pl.BlockSpec((bm, d), lambda i: (i, 0))
    vec = pl.BlockSpec((1, d), lambda i: (0, 0))
    return pl.pallas_call(
        _norm_kernel, grid=(s // bm,), in_specs=[row, vec], out_specs=row,
        out_shape=jax.ShapeDtypeStruct((s, d), BF16),
        compiler_params=_params("parallel"), name="rmsnorm")(x, g.reshape(1, d))


def _residual_norm(x, y, g_post, g_next):
    s, d = x.shape
    bm = _pick(s, (256, 128, 8))
    row = pl.BlockSpec((bm, d), lambda i: (i, 0))
    vec = pl.BlockSpec((1, d), lambda i: (0, 0))
    if g_next is None:
        return pl.pallas_call(
            _res_kernel, grid=(s // bm,), in_specs=[row, row, vec], out_specs=row,
            out_shape=jax.ShapeDtypeStruct((s, d), F32),
            compiler_params=_params("parallel"), name="residual")(x, y, g_post.reshape(1, d)), None
    return pl.pallas_call(
        _resnorm_kernel, grid=(s // bm,), in_specs=[row, row, vec, vec], out_specs=[row, row],
        out_shape=[jax.ShapeDtypeStruct((s, d), F32), jax.ShapeDtypeStruct((s, d), BF16)],
        compiler_params=_params("parallel"), name="residual_norm")(
            x, y, g_post.reshape(1, d), g_next.reshape(1, d))


def _mm_kernel(a_ref, w_ref, o_ref, *, sigmoid):
    acc = jnp.dot(a_ref[...], w_ref[...], preferred_element_type=F32)
    if sigmoid:
        acc = jax.nn.sigmoid(acc)
    o_ref[...] = acc.astype(o_ref.dtype)


def _mm_colscale_kernel(a_ref, w_ref, s_ref, o_ref):
    acc = jnp.dot(a_ref[...], w_ref[...], preferred_element_type=F32)
    o_ref[...] = (acc * s_ref[...]).astype(o_ref.dtype)


def _matmul_colscale(a, w, col_scale, out_dtype, *, bm, bn, name):
    m, k = a.shape
    n = w.shape[1]
    return pl.pallas_call(
        _mm_colscale_kernel,
        grid=(m // bm, n // bn),
        in_specs=[pl.BlockSpec((bm, k), lambda i, j: (i, 0)),
                  pl.BlockSpec((k, bn), lambda i, j: (0, j)),
                  pl.BlockSpec((1, bn), lambda i, j: (0, j))],
        out_specs=pl.BlockSpec((bm, bn), lambda i, j: (i, j)),
        out_shape=jax.ShapeDtypeStruct((m, n), out_dtype),
        compiler_params=_params("parallel", "arbitrary"), name=name)(a, w, col_scale.reshape(1, n))


def _matmul(a, w, out_dtype, *, bm, bn, sigmoid=False, name):
    m, k = a.shape
    n = w.shape[1]
    return pl.pallas_call(
        functools.partial(_mm_kernel, sigmoid=sigmoid),
        grid=(m // bm, n // bn),
        in_specs=[pl.BlockSpec((bm, k), lambda i, j: (i, 0)),
                  pl.BlockSpec((k, bn), lambda i, j: (0, j))],
        out_specs=pl.BlockSpec((bm, bn), lambda i, j: (i, j)),
        out_shape=jax.ShapeDtypeStruct((m, n), out_dtype),
        compiler_params=_params("parallel", "arbitrary"), name=name)(a, w)


def _merge_kernel(ya_ref, yb_ref, yc_ref, wa_ref, wb_ref, wc_ref, ga_ref, gb_ref, gc_ref, o_ref):
    pa = jnp.dot(ya_ref[...], wa_ref[...], preferred_element_type=F32)
    pb = jnp.dot(yb_ref[...], wb_ref[...], preferred_element_type=F32)
    pc = jnp.dot(yc_ref[...], wc_ref[...], preferred_element_type=F32)
    merged = (ga_ref[...].astype(F32) * pa + gb_ref[...].astype(F32) * pb
              + gc_ref[...].astype(F32) * pc)
    o_ref[...] = merged.astype(o_ref.dtype)


def _merge(ya, yb, yc, wa, wb, wc, gates, *, bm, bn):
    s = ya.shape[0]
    d = wa.shape[1]
    nj = d // bn

    def act(w):
        return pl.BlockSpec((bm, w), lambda i, j: (i, 0))

    def wgt(k):
        return pl.BlockSpec((k, bn), lambda i, j: (0, j))

    def gate(b):
        return pl.BlockSpec((bm, bn), lambda i, j: (i, b * nj + j))

    return pl.pallas_call(
        _merge_kernel, grid=(s // bm, nj),
        in_specs=[act(ya.shape[1]), act(yb.shape[1]), act(yc.shape[1]),
                  wgt(wa.shape[0]), wgt(wb.shape[0]), wgt(wc.shape[0]),
                  gate(0), gate(1), gate(2)],
        out_specs=pl.BlockSpec((bm, bn), lambda i, j: (i, j)),
        out_shape=jax.ShapeDtypeStruct((s, d), BF16),
        compiler_params=_params("parallel", "arbitrary"), name="branch_merge")(
            ya, yb, yc, wa, wb, wc, gates, gates, gates)


def _ffn_in_kernel(h_ref, wg_ref, wu_ref, o_ref):
    h = h_ref[...]
    g = jnp.dot(h, wg_ref[...], preferred_element_type=F32)
    u = jnp.dot(h, wu_ref[...], preferred_element_type=F32)
    o_ref[...] = (g * jax.nn.sigmoid(g) * u).astype(o_ref.dtype)


def _ffn_in(h, w, *, bm, bn):
    s, d = h.shape
    d_ff = w.shape[1] // 2
    nj = d_ff // bn
    return pl.pallas_call(
        _ffn_in_kernel, grid=(s // bm, nj),
        in_specs=[pl.BlockSpec((bm, d), lambda i, j: (i, 0)),
                  pl.BlockSpec((d, bn), lambda i, j: (0, j)),
                  pl.BlockSpec((d, bn), lambda i, j: (0, nj + j))],
        out_specs=pl.BlockSpec((bm, bn), lambda i, j: (i, j)),
        out_shape=jax.ShapeDtypeStruct((s, d_ff), BF16),
        compiler_params=_params("parallel", "arbitrary"), name="ffn_in")(h, w, w)


def _causal_conv(x, w_ref, xbuf, width):
    t = x.shape[0]
    xbuf[CONV_HALO:CONV_HALO + t, :] = x
    out = None
    for k in range(width):
        off = CONV_HALO - (width - 1) + k
        term = w_ref[k:k + 1, :] * xbuf[off:off + t, :]
        out = term if out is None else out + term
    xbuf[0:CONV_HALO, :] = xbuf[t:t + CONV_HALO, :]
    return out


def _linear_scan(a, u):
    t = a.shape[0]
    row = lax.broadcasted_iota(jnp.int32, a.shape, 0)
    s = 1
    while s < t:
        keep = row >= s
        a_prev = pltpu.roll(a, s, 0)
        u_prev = pltpu.roll(u, s, 0)
        u = jnp.where(keep, a * u_prev + u, u)
        a = jnp.where(keep, a * a_prev, a)
        s *= 2
    return a, u


def _mix_a_kernel(xa_ref, ga_ref, cw_ref, cb_ref, wa_ref, ba_ref, wx_ref, bx_ref, lam_ref,
                  o_ref, xbuf, hcar):
    @pl.when(pl.program_id(1) == 0)
    def _():
        xbuf[0:CONV_HALO, :] = jnp.zeros((CONV_HALO, xbuf.shape[1]), F32)
        hcar[...] = jnp.zeros_like(hcar)

    t, c = xa_ref.shape
    x = _causal_conv(xa_ref[...].astype(F32), cw_ref, xbuf, cw_ref.shape[0]) + cb_ref[...]
    xb = x.astype(BF16)
    r_parts, i_parts = [], []
    for g in range(c // LRU_BLOCK_DIM):
        xg = xb[:, g * LRU_BLOCK_DIM:(g + 1) * LRU_BLOCK_DIM]
        r_parts.append(jnp.dot(xg, wa_ref[g], preferred_element_type=F32))
        i_parts.append(jnp.dot(xg, wx_ref[g], preferred_element_type=F32))
    r = jax.nn.sigmoid(jnp.concatenate(r_parts, axis=1) + ba_ref[...])
    i = jax.nn.sigmoid(jnp.concatenate(i_parts, axis=1) + bx_ref[...])
    z = -lam_ref[...]
    softplus = jnp.maximum(z, 0.0) + jnp.log1p(jnp.exp(-jnp.abs(z)))
    log_a = (-LRU_C) * r * softplus
    a = jnp.exp(log_a)
    th = jnp.tanh(log_a)
    u = jnp.sqrt(-2.0 * th / (1.0 - th)) * (i * x)
    a_cum, h = _linear_scan(a, u)
    h = h + a_cum * hcar[...]
    hcar[...] = h[t - 1:t, :]
    o_ref[...] = (jax.nn.gelu(ga_ref[...].astype(F32)) * h).astype(o_ref.dtype)


def _mix_a(proj, col_x, col_g, conv_w, conv_b, w_a, b_a, w_x, b_x, lam):
    s = proj.shape[0]
    width = conv_w.shape[1]
    tc = CHAN_TILE
    tt = _pick(s, (TIME_TILE,))
    gpt = tc // LRU_BLOCK_DIM
    x_off, g_off = col_x // tc, col_g // tc
    vec = pl.BlockSpec((1, tc), lambda c, t: (0, c))
    blk = pl.BlockSpec((gpt, LRU_BLOCK_DIM, LRU_BLOCK_DIM), lambda c, t: (c, 0, 0))
    return pl.pallas_call(
        _mix_a_kernel, grid=(width // tc, s // tt),
        in_specs=[pl.BlockSpec((tt, tc), lambda c, t: (t, x_off + c)),
                  pl.BlockSpec((tt, tc), lambda c, t: (t, g_off + c)),
                  pl.BlockSpec((conv_w.shape[0], tc), lambda c, t: (0, c)),
                  vec, blk, vec, blk, vec, vec],
        out_specs=pl.BlockSpec((tt, tc), lambda c, t: (t, c)),
        out_shape=jax.ShapeDtypeStruct((s, width), BF16),
        scratch_shapes=[pltpu.VMEM((tt + CONV_HALO, tc), F32), pltpu.VMEM((1, tc), F32)],
        compiler_params=_params("arbitrary", "arbitrary"), name="mixer_rglru")(
            proj, proj, conv_w, conv_b.reshape(1, width), w_a.astype(BF16), b_a.reshape(1, width),
            w_x.astype(BF16), b_x.reshape(1, width), lam.reshape(1, width))


def _mix_c_kernel(sb_ref, sc_ref, sx_ref, cw_ref, o_ref, xbuf):
    @pl.when(pl.program_id(1) == 0)
    def _():
        xbuf[0:CONV_HALO, :] = jnp.zeros((CONV_HALO, xbuf.shape[1]), F32)

    z = sc_ref[...].astype(F32) * sx_ref[...].astype(F32)
    conv = _causal_conv(z, cw_ref, xbuf, cw_ref.shape[0])
    o_ref[...] = (sb_ref[...].astype(F32) * conv).astype(o_ref.dtype)


def _mix_c(proj, col_b, col_c, col_x, conv_w):
    s = proj.shape[0]
    width = conv_w.shape[1]
    tc = CHAN_TILE
    tt = _pick(s, (TIME_TILE,))

    def col(off):
        return pl.BlockSpec((tt, tc), lambda c, t: (t, off // tc + c))

    return pl.pallas_call(
        _mix_c_kernel, grid=(width // tc, s // tt),
        in_specs=[col(col_b), col(col_c), col(col_x),
                  pl.BlockSpec((conv_w.shape[0], tc), lambda c, t: (0, c))],
        out_specs=pl.BlockSpec((tt, tc), lambda c, t: (t, c)),
        out_shape=jax.ShapeDtypeStruct((s, width), BF16),
        scratch_shapes=[pltpu.VMEM((tt + CONV_HALO, tc), F32)],
        compiler_params=_params("arbitrary", "arbitrary"), name="mixer_shortconv")(
            proj, proj, proj, conv_w)


def _t5_bucket_np(dist):
    n = np.maximum(dist, 0)
    max_exact = REL_BUCKETS // 2
    nf = np.maximum(n, 1).astype(np.float32)
    large = max_exact + (np.log(nf / np.float32(max_exact)) / np.float32(math.log(REL_MAX_DIST / max_exact))
                         * np.float32(REL_BUCKETS - max_exact)).astype(np.int32)
    large = np.minimum(large, REL_BUCKETS - 1)
    return np.where(n < max_exact, n, large).astype(np.int32)


def _near_offsets():
    d = 0
    while True:
        lo = d * MOBA_BLOCK - (MOBA_BLOCK - 1)
        if lo > 0 and _t5_bucket_np(np.array([lo]))[0] == REL_BUCKETS - 1:
            return d
        d += 1


N_NEAR = _near_offsets()


KEY_GROUP = 4


def _strip_deltas(group):
    return list(range(N_NEAR - 1 + 2 * group - 1, -group, -1))


def _bucket_strip(group):
    kr = np.arange(MOBA_BLOCK)[:, None]
    qc = np.arange(MOBA_BLOCK)[None, :]
    tiles = []
    for d in _strip_deltas(group):
        rel = d * MOBA_BLOCK + qc - kr
        tiles.append(np.where(rel >= 0, _t5_bucket_np(rel), REL_BUCKETS))
    return np.stack(tiles).astype(np.int32)


def _bias_kernel(tbl_ref, bucket_ref, o_ref):
    h = pl.program_id(0)
    bucket = bucket_ref[0]
    acc = jnp.full(bucket.shape, NEG, F32)
    for b in range(REL_BUCKETS):
        acc = jnp.where(bucket == b, tbl_ref[h * REL_BUCKETS + b], acc)
    o_ref[0] = acc


def _bias_strip(tbl, n_heads, group):
    buckets = jnp.asarray(_bucket_strip(group))
    n_strip = buckets.shape[0]
    blk = MOBA_BLOCK
    return pl.pallas_call(
        _bias_kernel, grid=(n_heads, n_strip),
        in_specs=[pl.BlockSpec(memory_space=pltpu.SMEM),
                  pl.BlockSpec((1, blk, blk), lambda h, d: (d, 0, 0))],
        out_specs=pl.BlockSpec((1, blk, blk), lambda h, d: (h, d, 0)),
        out_shape=jax.ShapeDtypeStruct((n_heads, n_strip * blk, blk), F32),
        compiler_params=_params("arbitrary", "arbitrary"), name="t5_bias_strip")(tbl, buckets)


def _kmean_kernel(k_ref, o_ref):
    k = k_ref[...].astype(F32)
    nb = k.shape[0] // MOBA_BLOCK
    o_ref[0] = jnp.mean(k.reshape(nb, MOBA_BLOCK, k.shape[1]), axis=1)


def _kmean(proj, col_k, n_heads):
    s = proj.shape[0]
    nb = s // MOBA_BLOCK
    off = col_k // HEAD_DIM
    return pl.pallas_call(
        _kmean_kernel, grid=(n_heads,),
        in_specs=[pl.BlockSpec((s, HEAD_DIM), lambda h: (0, off + h))],
        out_specs=pl.BlockSpec((1, nb, HEAD_DIM), lambda h: (h, 0, 0)),
        out_shape=jax.ShapeDtypeStruct((n_heads, nb, HEAD_DIM), F32),
        compiler_params=_params("parallel"), name="moba_kmean")(proj)


_NT = (((1,), (1,)), ((), ()))


def _attn_kernel(q_ref, k_ref, vt_ref, km_ref, bias_ref, o_ref, neg_ref, acc_ref, *, group):
    i = pl.program_id(1)
    blk = MOBA_BLOCK
    gk = group * blk
    nb = neg_ref.shape[0]
    q = q_ref[...]

    km = km_ref[0]
    km_hi = km.astype(BF16)
    km_lo = (km - km_hi.astype(F32)).astype(BF16)
    gate = (lax.dot_general(km_hi, q, _NT, preferred_element_type=F32)
            + lax.dot_general(km_lo, q, _NT, preferred_element_type=F32))
    bidx = lax.broadcasted_iota(jnp.int32, (nb, blk), 0).astype(F32)
    own = i.astype(F32)
    past = bidx < own
    g = jnp.where(past, gate, NEG)
    chosen = jnp.zeros((nb, blk), jnp.bool_)
    for _ in range(MOBA_TOPK):
        mx = jnp.max(g, axis=0, keepdims=True)
        first = jnp.min(jnp.where(g == mx, bidx, float(nb)), axis=0, keepdims=True)
        pick = bidx == first
        chosen = jnp.logical_or(chosen, pick)
        g = jnp.where(pick, -jnp.inf, g)
    attend = jnp.logical_or(jnp.logical_and(chosen, past), bidx == own)
    neg_ref[...] = jnp.where(attend, 0.0, NEG)
    acc_ref[...] = jnp.zeros_like(acc_ref)

    def scores(gi):
        hk = gk // 2
        halves = []
        for part in range(2):
            kb = k_ref[pl.ds(pl.multiple_of(gi * gk + part * hk, hk), hk), :]
            halves.append(lax.dot_general(kb, q, _NT, preferred_element_type=F32))
        return jnp.concatenate(halves, axis=0)

    def update(s, gi, valid, m, acc):
        top = jnp.maximum(_strip_deltas(group)[0] - (i - gi * group), 0)
        s = s + bias_ref[0, pl.ds(pl.multiple_of(top * blk, blk), gk), :]
        pieces, rows = [], []
        m_new = m
        for b in range(group):
            row = neg_ref[pl.ds(gi * group + b, 1), :]
            if valid is not None:
                row = jnp.where(valid, row, NEG)
            piece = s[b * blk:(b + 1) * blk]
            m_new = jnp.maximum(m_new, jnp.max(piece, axis=0, keepdims=True) + row)
            pieces.append(piece)
            rows.append(row)
        probs = [jnp.exp2(piece - (m_new - row)).astype(BF16) for piece, row in zip(pieces, rows)]
        p = jnp.concatenate(probs, axis=0)
        alpha = jnp.exp2(m - m_new)
        hk = gk // 2
        pv = (jnp.dot(vt_ref[0, gi, :, 0:hk], p[0:hk], preferred_element_type=F32)
              + jnp.dot(vt_ref[0, gi, :, hk:gk], p[hk:gk], preferred_element_type=F32))
        acc[...] = acc[...] * alpha + pv
        return m_new

    acc_a, acc_b = acc_ref.at[0], acc_ref.at[1]
    g_own = i // group
    m0 = update(scores(g_own), g_own, None, jnp.full((1, blk), NEG, F32), acc_a)

    def pair(t, carry):
        m_a, m_b = carry
        g_a = g_own - 1 - 2 * t
        g_b = jnp.maximum(g_a - 1, 0)
        s_a = scores(g_a)
        s_b = scores(g_b)
        m_a = update(s_a, g_a, None, m_a, acc_a)
        m_b = update(s_b, g_b, g_a >= 1, m_b, acc_b)
        return m_a, m_b

    m_a, m_b = lax.fori_loop(0, (g_own + 1) // 2, pair, (m0, m0))
    m = jnp.maximum(m_a, m_b)
    merged = acc_a[...] * jnp.exp2(m_a - m) + acc_b[...] * jnp.exp2(m_b - m)
    dh = o_ref.shape[1]
    out = merged[0:dh] * (1.0 / merged[dh:dh + 1])
    o_ref[...] = out.T.astype(o_ref.dtype)


def _moba(proj, col_q, col_k, col_v, n_heads, rel_bias):
    s = proj.shape[0]
    blk, dh = MOBA_BLOCK, HEAD_DIM
    nb = s // blk
    group = KEY_GROUP if nb % KEY_GROUP == 0 else 1
    tbl = (rel_bias.astype(F32).T * LOG2E).reshape(-1)
    bias = _bias_strip(tbl, n_heads, group)
    km = _kmean(proj, col_k, n_heads)
    v = proj[:, col_v:col_v + n_heads * dh]
    vt = v.reshape(nb // group, group * blk, n_heads, dh).transpose(2, 0, 3, 1)
    ones = jnp.zeros((n_heads, nb // group, BF16_SUBLANES, group * blk), BF16).at[:, :, 0, :].set(1.0)
    vt = jnp.concatenate([vt, ones], axis=2)
    vrows = dh + BF16_SUBLANES
    q_off, k_off = col_q // dh, col_k // dh
    return pl.pallas_call(
        functools.partial(_attn_kernel, group=group), grid=(n_heads, nb),
        in_specs=[pl.BlockSpec((blk, dh), lambda h, i: (i, q_off + h)),
                  pl.BlockSpec((s, dh), lambda h, i: (0, k_off + h)),
                  pl.BlockSpec((1, nb // group, vrows, group * blk), lambda h, i: (h, 0, 0, 0)),
                  pl.BlockSpec((1, nb, dh), lambda h, i: (h, 0, 0)),
                  pl.BlockSpec((1, bias.shape[1], blk), lambda h, i: (h, 0, 0))],
        out_specs=pl.BlockSpec((blk, dh), lambda h, i: (i, h)),
        out_shape=jax.ShapeDtypeStruct((s, n_heads * dh), BF16),
        scratch_shapes=[pltpu.VMEM((nb, blk), F32), pltpu.VMEM((2, vrows, blk), F32)],
        compiler_params=_params("parallel", "arbitrary"), name="moba_attention")(
            proj, proj, vt, km, bias)


def _layer(x, h, rel_bias, g_post_mix, g_pre_ffn, g_post_ffn, g_next, w_in, conv_a_w, conv_a_b,
           lru_wa, lru_ba, lru_wx, lru_bx, lru_lambda, conv_c_w, w_branch, w_gate, w_out,
           w_ffn_in, w_ffn_out):
    s, d = x.shape
    lru_w = conv_a_w.shape[1]
    sc_w = conv_c_w.shape[1]
    att_w = (w_in.shape[1] - 2 * lru_w - 3 * sc_w) // 3
    n_heads = att_w // HEAD_DIM
    col_ga = lru_w
    col_q = 2 * lru_w
    col_k = col_q + att_w
    col_v = col_k + att_w
    col_sb = col_v + att_w
    col_sc = col_sb + sc_w
    col_sx = col_sc + sc_w

    bm = _pick(s, (1024, 512, 256))
    col = jnp.arange(w_in.shape[1])
    col_scale = jnp.where((col >= col_q) & (col < col_k), QK_SCALE, 1.0).astype(F32)
    proj = _matmul_colscale(h, w_in.astype(BF16), col_scale, BF16, bm=bm,
                            bn=_pick(w_in.shape[1], (768, 512, 256, 128)), name="in_proj")
    gates = _matmul(h, w_gate.astype(BF16), BF16, bm=bm, bn=_pick(w_gate.shape[1], (768, 512, 256, 128)),
                    sigmoid=True, name="gate_proj")

    ya = _mix_a(proj, 0, col_ga, conv_a_w, conv_a_b, lru_wa, lru_ba, lru_wx, lru_bx, lru_lambda)
    yb = _moba(proj, col_q, col_k, col_v, n_heads, rel_bias)
    yc = _mix_c(proj, col_sb, col_sc, col_sx, conv_c_w)

    wb = w_branch.astype(BF16)
    merged = _merge(ya, yb, yc, wb[:lru_w], wb[lru_w:lru_w + att_w], wb[lru_w + att_w:], gates,
                    bm=bm, bn=_pick(d, (512, 256, 128)))
    mix = _matmul(merged, w_out.astype(BF16), F32, bm=bm, bn=_pick(d, (512, 256, 128)), name="out_proj")
    x, h2 = _residual_norm(x, mix, g_post_mix, g_pre_ffn)

    act = _ffn_in(h2, w_ffn_in.astype(BF16), bm=bm, bn=_pick(w_ffn_in.shape[1] // 2, (256, 128)))
    f = _matmul(act, w_ffn_out.astype(BF16), F32, bm=_pick(s, (512, 256)), bn=_pick(d, (512, 256, 128)),
                name="ffn_out")
    return _residual_norm(x, f, g_post_ffn, g_next)


@jax.jit
def _forward(x, rel_bias, norm_pre_mix, norm_post_mix, norm_pre_ffn, norm_post_ffn, w_in, conv_a_w,
             conv_a_b, lru_wa, lru_ba, lru_wx, lru_bx, lru_lambda, conv_c_w, w_branch, w_gate, w_out,
             w_ffn_in, w_ffn_out):
    bsz, s, d = x.shape
    depth = w_in.shape[0]
    outs = []
    for b in range(bsz):
        xb = x[b]
        h = _norm(xb, norm_pre_mix[0])
        for l in range(depth):
            g_next = norm_pre_mix[l + 1] if l + 1 < depth else None
            xb, h = _layer(xb, h, rel_bias, norm_post_mix[l], norm_pre_ffn[l], norm_post_ffn[l], g_next,
                           w_in[l], conv_a_w[l], conv_a_b[l], lru_wa[l], lru_ba[l], lru_wx[l], lru_bx[l],
                           lru_lambda[l], conv_c_w[l], w_branch[l], w_gate[l], w_out[l], w_ffn_in[l],
                           w_ffn_out[l])
        outs.append(xb)
    return jnp.stack(outs)


def kernel(x, rel_bias, norm_pre_mix, norm_post_mix, norm_pre_ffn, norm_post_ffn, w_in, conv_a_w, conv_a_b, lru_wa, lru_ba, lru_wx, lru_bx, lru_lambda, conv_c_w, w_branch, w_gate, w_out, w_ffn_in, w_ffn_out):
    return _forward(x, rel_bias, norm_pre_mix, norm_post_mix, norm_pre_ffn, norm_post_ffn, w_in, conv_a_w,
                    conv_a_b, lru_wa, lru_ba, lru_wx, lru_bx, lru_lambda, conv_c_w, w_branch, w_gate, w_out,
                    w_ffn_in, w_ffn_out)
```

```python
import functools
import math

import numpy as np
import jax
import jax.numpy as jnp
from jax import lax
from jax.experimental import pallas as pl
from jax.experimental.pallas import tpu as pltpu

F32 = jnp.float32
BF16 = jnp.bfloat16

EPS = 1e-6
NEG = -1e30
LOG2E = 1.4426950408889634

HEAD_DIM = 128
QK_SCALE = HEAD_DIM ** -0.5 * LOG2E
MOBA_BLOCK = 256
MOBA_TOPK = 3
REL_BUCKETS = 32
REL_MAX_DIST = 2048
LRU_BLOCK_DIM = 128
LRU_C = 8.0
N_BRANCH = 3
CHAN_TILE = 512
TIME_TILE = 256
CONV_HALO = 8
BF16_SUBLANES = 16
VMEM_LIMIT = 56 * 1024 * 1024


def _pick(n, candidates):
    for c in candidates:
        if n % c == 0:
            return c
    raise ValueError(f"no tile in {candidates} divides {n}")


def _params(*sem):
    return pltpu.CompilerParams(dimension_semantics=sem, vmem_limit_bytes=VMEM_LIMIT)


def _rms(x, g):
    return x * lax.rsqrt(jnp.mean(x * x, axis=-1, keepdims=True) + EPS) * g


def _norm_kernel(x_ref, g_ref, h_ref):
    h_ref[...] = _rms(x_ref[...], g_ref[...]).astype(h_ref.dtype)


def _resnorm_kernel(x_ref, y_ref, gp_ref, gn_ref, xo_ref, h_ref):
    xn = x_ref[...] + _rms(y_ref[...].astype(F32), gp_ref[...])
    xo_ref[...] = xn
    h_ref[...] = _rms(xn, gn_ref[...]).astype(h_ref.dtype)


def _res_kernel(x_ref, y_ref, gp_ref, xo_ref):
    xo_ref[...] = x_ref[...] + _rms(y_ref[...].astype(F32), gp_ref[...])


def _norm(x, g):
    s, d = x.shape
    bm = _pick(s, (256, 128, 8))
    row = pl.BlockSpec((bm, d), lambda i: (i, 0))
    vec = pl.BlockSpec((1, d), lambda i: (0, 0))
    return pl.pallas_call(
        _norm_kernel, grid=(s // bm,), in_specs=[row, vec], out_specs=row,
        out_shape=jax.ShapeDtypeStruct((s, d), BF16),
        compiler_params=_params("parallel"), name="rmsnorm")(x, g.reshape(1, d))


def _residual_norm(x, y, g_post, g_next):
    s, d = x.shape
    bm = _pick(s, (256, 128, 8))
    row = pl.BlockSpec((bm, d), lambda i: (i, 0))
    vec = pl.BlockSpec((1, d), lambda i: (0, 0))
    if g_next is None:
        return pl.pallas_call(
            _res_kernel, grid=(s // bm,), in_specs=[row, row, vec], out_specs=row,
            out_shape=jax.ShapeDtypeStruct((s, d), F32),
            compiler_params=_params("parallel"), name="residual")(x, y, g_post.reshape(1, d)), None
    return pl.pallas_call(
        _resnorm_kernel, grid=(s // bm,), in_specs=[row, row, vec, vec], out_specs=[row, row],
        out_shape=[jax.ShapeDtypeStruct((s, d), F32), jax.ShapeDtypeStruct((s, d), BF16)],
        compiler_params=_params("parallel"), name="residual_norm")(
            x, y, g_post.reshape(1, d), g_next.reshape(1, d))


def _mm_kernel(a_ref, w_ref, o_ref, *, sigmoid):
    acc = jnp.dot(a_ref[...], w_ref[...], preferred_element_type=F32)
    if sigmoid:
        acc = jax.nn.sigmoid(acc)
    o_ref[...] = acc.astype(o_ref.dtype)


def _mm_colscale_kernel(a_ref, w_ref, s_ref, o_ref):
    acc = jnp.dot(a_ref[...], w_ref[...], preferred_element_type=F32)
    o_ref[...] = (acc * s_ref[...]).astype(o_ref.dtype)


def _matmul_colscale(a, w, col_scale, out_dtype, *, bm, bn, name):
    m, k = a.shape
    n = w.shape[1]
    return pl.pallas_call(
        _mm_colscale_kernel,
        grid=(m // bm, n // bn),
        in_specs=[pl.BlockSpec((bm, k), lambda i, j: (i, 0)),
                  pl.BlockSpec((k, bn), lambda i, j: (0, j)),
                  pl.BlockSpec((1, bn), lambda i, j: (0, j))],
        out_specs=pl.BlockSpec((bm, bn), lambda i, j: (i, j)),
        out_shape=jax.ShapeDtypeStruct((m, n), out_dtype),
        compiler_params=_params("parallel", "arbitrary"), name=name)(a, w, col_scale.reshape(1, n))


def _matmul(a, w, out_dtype, *, bm, bn, sigmoid=False, name):
    m, k = a.shape
    n = w.shape[1]
    return pl.pallas_call(
        functools.partial(_mm_kernel, sigmoid=sigmoid),
        grid=(m // bm, n // bn),
        in_specs=[pl.BlockSpec((bm, k), lambda i, j: (i, 0)),
                  pl.BlockSpec((k, bn), lambda i, j: (0, j))],
        out_specs=pl.BlockSpec((bm, bn), lambda i, j: (i, j)),
        out_shape=jax.ShapeDtypeStruct((m, n), out_dtype),
        compiler_params=_params("parallel", "arbitrary"), name=name)(a, w)


def _merge_kernel(ya_ref, yb_ref, yc_ref, wa_ref, wb_ref, wc_ref, ga_ref, gb_ref, gc_ref, o_ref):
    pa = jnp.dot(ya_ref[...], wa_ref[...], preferred_element_type=F32)
    pb = jnp.dot(yb_ref[...], wb_ref[...], preferred_element_type=F32)
    pc = jnp.dot(yc_ref[...], wc_ref[...], preferred_element_type=F32)
    merged = (ga_ref[...].astype(F32) * pa + gb_ref[...].astype(F32) * pb
              + gc_ref[...].astype(F32) * pc)
    o_ref[...] = merged.astype(o_ref.dtype)


def _merge(ya, yb, yc, w, gates, *, bm, bn):
    s = ya.shape[0]
    d = w.shape[1]
    nj = d // bn
    ka, kb, kc = ya.shape[1], yb.shape[1], yc.shape[1]
    assert ka % kb == 0 and (ka + kb) % kc == 0, "branch widths must tile the rows of w_branch"

    def act(k):
        return pl.BlockSpec((bm, k), lambda i, j: (i, 0))

    def wgt(k, row_off):
        return pl.BlockSpec((k, bn), lambda i, j: (row_off // k, j))

    def gate(b):
        return pl.BlockSpec((bm, bn), lambda i, j: (i, b * nj + j))

    return pl.pallas_call(
        _merge_kernel, grid=(s // bm, nj),
        in_specs=[act(ka), act(kb), act(kc), wgt(ka, 0), wgt(kb, ka), wgt(kc, ka + kb),
                  gate(0), gate(1), gate(2)],
        out_specs=pl.BlockSpec((bm, bn), lambda i, j: (i, j)),
        out_shape=jax.ShapeDtypeStruct((s, d), BF16),
        compiler_params=_params("parallel", "arbitrary"), name="branch_merge")(
            ya, yb, yc, w, w, w, gates, gates, gates)


def _ffn_in_kernel(h_ref, wg_ref, wu_ref, o_ref):
    h = h_ref[...]
    g = jnp.dot(h, wg_ref[...], preferred_element_type=F32)
    u = jnp.dot(h, wu_ref[...], preferred_element_type=F32)
    o_ref[...] = (g * jax.nn.sigmoid(g) * u).astype(o_ref.dtype)


def _ffn_in(h, w, *, bm, bn):
    s, d = h.shape
    d_ff = w.shape[1] // 2
    nj = d_ff // bn
    return pl.pallas_call(
        _ffn_in_kernel, grid=(s // bm, nj),
        in_specs=[pl.BlockSpec((bm, d), lambda i, j: (i, 0)),
                  pl.BlockSpec((d, bn), lambda i, j: (0, j)),
                  pl.BlockSpec((d, bn), lambda i, j: (0, nj + j))],
        out_specs=pl.BlockSpec((bm, bn), lambda i, j: (i, j)),
        out_shape=jax.ShapeDtypeStruct((s, d_ff), BF16),
        compiler_params=_params("parallel", "arbitrary"), name="ffn_in")(h, w, w)


def _causal_conv(x, w_ref, xbuf, width):
    t = x.shape[0]
    xbuf[CONV_HALO:CONV_HALO + t, :] = x
    out = None
    for k in range(width):
        off = CONV_HALO - (width - 1) + k
        term = w_ref[k:k + 1, :] * xbuf[off:off + t, :]
        out = term if out is None else out + term
    xbuf[0:CONV_HALO, :] = xbuf[t:t + CONV_HALO, :]
    return out


def _linear_scan(a, u):
    t = a.shape[0]
    row = lax.broadcasted_iota(jnp.int32, a.shape, 0)
    s = 1
    while s < t:
        keep = row >= s
        a_prev = pltpu.roll(a, s, 0)
        u_prev = pltpu.roll(u, s, 0)
        u = jnp.where(keep, a * u_prev + u, u)
        a = jnp.where(keep, a * a_prev, a)
        s *= 2
    return a, u


def _mix_a_kernel(xa_ref, ga_ref, cw_ref, cb_ref, wa_ref, ba_ref, wx_ref, bx_ref, lam_ref,
                  o_ref, xbuf, hcar):
    @pl.when(pl.program_id(1) == 0)
    def _():
        xbuf[0:CONV_HALO, :] = jnp.zeros((CONV_HALO, xbuf.shape[1]), F32)
        hcar[...] = jnp.zeros_like(hcar)

    t, c = xa_ref.shape
    x = _causal_conv(xa_ref[...].astype(F32), cw_ref, xbuf, cw_ref.shape[0]) + cb_ref[...]
    xb = x.astype(BF16)
    r_parts, i_parts = [], []
    for g in range(c // LRU_BLOCK_DIM):
        xg = xb[:, g * LRU_BLOCK_DIM:(g + 1) * LRU_BLOCK_DIM]
        r_parts.append(jnp.dot(xg, wa_ref[g], preferred_element_type=F32))
        i_parts.append(jnp.dot(xg, wx_ref[g], preferred_element_type=F32))
    r = jax.nn.sigmoid(jnp.concatenate(r_parts, axis=1) + ba_ref[...])
    i = jax.nn.sigmoid(jnp.concatenate(i_parts, axis=1) + bx_ref[...])
    z = -lam_ref[...]
    softplus = jnp.maximum(z, 0.0) + jnp.log1p(jnp.exp(-jnp.abs(z)))
    log_a = (-LRU_C) * r * softplus
    a = jnp.exp(log_a)
    th = jnp.tanh(log_a)
    u = jnp.sqrt(-2.0 * th / (1.0 - th)) * (i * x)
    a_cum, h = _linear_scan(a, u)
    h = h + a_cum * hcar[...]
    hcar[...] = h[t - 1:t, :]
    o_ref[...] = (jax.nn.gelu(ga_ref[...].astype(F32)) * h).astype(o_ref.dtype)


def _mix_a(proj, col_x, col_g, conv_w, conv_b, w_a, b_a, w_x, b_x, lam):
    s = proj.shape[0]
    width = conv_w.shape[1]
    tc = CHAN_TILE
    tt = _pick(s, (TIME_TILE,))
    gpt = tc // LRU_BLOCK_DIM
    x_off, g_off = col_x // tc, col_g // tc
    vec = pl.BlockSpec((1, tc), lambda c, t: (0, c))
    blk = pl.BlockSpec((gpt, LRU_BLOCK_DIM, LRU_BLOCK_DIM), lambda c, t: (c, 0, 0))
    return pl.pallas_call(
        _mix_a_kernel, grid=(width // tc, s // tt),
        in_specs=[pl.BlockSpec((tt, tc), lambda c, t: (t, x_off + c)),
                  pl.BlockSpec((tt, tc), lambda c, t: (t, g_off + c)),
                  pl.BlockSpec((conv_w.shape[0], tc), lambda c, t: (0, c)),
                  vec, blk, vec, blk, vec, vec],
        out_specs=pl.BlockSpec((tt, tc), lambda c, t: (t, c)),
        out_shape=jax.ShapeDtypeStruct((s, width), BF16),
        scratch_shapes=[pltpu.VMEM((tt + CONV_HALO, tc), F32), pltpu.VMEM((1, tc), F32)],
        compiler_params=_params("arbitrary", "arbitrary"), name="mixer_rglru")(
            proj, proj, conv_w, conv_b.reshape(1, width), w_a.astype(BF16), b_a.reshape(1, width),
            w_x.astype(BF16), b_x.reshape(1, width), lam.reshape(1, width))


def _mix_c_kernel(sb_ref, sc_ref, sx_ref, cw_ref, o_ref, xbuf):
    @pl.when(pl.program_id(1) == 0)
    def _():
        xbuf[0:CONV_HALO, :] = jnp.zeros((CONV_HALO, xbuf.shape[1]), F32)

    z = sc_ref[...].astype(F32) * sx_ref[...].astype(F32)
    conv = _causal_conv(z, cw_ref, xbuf, cw_ref.shape[0])
    o_ref[...] = (sb_ref[...].astype(F32) * conv).astype(o_ref.dtype)


def _mix_c(proj, col_b, col_c, col_x, conv_w):
    s = proj.shape[0]
    width = conv_w.shape[1]
    tc = CHAN_TILE
    tt = _pick(s, (TIME_TILE,))

    def col(off):
        return pl.BlockSpec((tt, tc), lambda c, t: (t, off // tc + c))

    return pl.pallas_call(
        _mix_c_kernel, grid=(width // tc, s // tt),
        in_specs=[col(col_b), col(col_c), col(col_x),
                  pl.BlockSpec((conv_w.shape[0], tc), lambda c, t: (0, c))],
        out_specs=pl.BlockSpec((tt, tc), lambda c, t: (t, c)),
        out_shape=jax.ShapeDtypeStruct((s, width), BF16),
        scratch_shapes=[pltpu.VMEM((tt + CONV_HALO, tc), F32)],
        compiler_params=_params("arbitrary", "arbitrary"), name="mixer_shortconv")(
            proj, proj, proj, conv_w)


def _t5_bucket_np(dist):
    n = np.maximum(dist, 0)
    max_exact = REL_BUCKETS // 2
    nf = np.maximum(n, 1).astype(np.float32)
    large = max_exact + (np.log(nf / np.float32(max_exact)) / np.float32(math.log(REL_MAX_DIST / max_exact))
                         * np.float32(REL_BUCKETS - max_exact)).astype(np.int32)
    large = np.minimum(large, REL_BUCKETS - 1)
    return np.where(n < max_exact, n, large).astype(np.int32)


def _near_offsets():
    d = 0
    while True:
        lo = d * MOBA_BLOCK - (MOBA_BLOCK - 1)
        if lo > 0 and _t5_bucket_np(np.array([lo]))[0] == REL_BUCKETS - 1:
            return d
        d += 1


N_NEAR = _near_offsets()


KEY_GROUP = 4


def _strip_deltas(group):
    return list(range(N_NEAR - 1 + 2 * group - 1, -group, -1))


def _bucket_strip(group):
    kr = np.arange(MOBA_BLOCK)[:, None]
    qc = np.arange(MOBA_BLOCK)[None, :]
    tiles = []
    for d in _strip_deltas(group):
        rel = d * MOBA_BLOCK + qc - kr
        tiles.append(np.where(rel >= 0, _t5_bucket_np(rel), REL_BUCKETS))
    return np.stack(tiles).astype(np.int32)


def _bias_kernel(tbl_ref, bucket_ref, o_ref):
    h = pl.program_id(0)
    bucket = bucket_ref[0]
    acc = jnp.full(bucket.shape, NEG, F32)
    for b in range(REL_BUCKETS):
        acc = jnp.where(bucket == b, tbl_ref[h * REL_BUCKETS + b], acc)
    o_ref[0] = acc


def _bias_strip(rel_bias, group):
    n_heads = rel_bias.shape[1]
    tbl = (rel_bias.astype(F32).T * LOG2E).reshape(-1)
    buckets = jnp.asarray(_bucket_strip(group))
    n_strip = buckets.shape[0]
    blk = MOBA_BLOCK
    return pl.pallas_call(
        _bias_kernel, grid=(n_heads, n_strip),
        in_specs=[pl.BlockSpec(memory_space=pltpu.SMEM),
                  pl.BlockSpec((1, blk, blk), lambda h, d: (d, 0, 0))],
        out_specs=pl.BlockSpec((1, blk, blk), lambda h, d: (h, d, 0)),
        out_shape=jax.ShapeDtypeStruct((n_heads, n_strip * blk, blk), F32),
        compiler_params=_params("arbitrary", "arbitrary"), name="t5_bias_strip")(tbl, buckets)


def _kmean_kernel(k_ref, o_ref):
    k = k_ref[...].astype(F32)
    nb = k.shape[0] // MOBA_BLOCK
    o_ref[0] = jnp.mean(k.reshape(nb, MOBA_BLOCK, k.shape[1]), axis=1)


def _kmean(proj, col_k, n_heads):
    s = proj.shape[0]
    nb = s // MOBA_BLOCK
    off = col_k // HEAD_DIM
    return pl.pallas_call(
        _kmean_kernel, grid=(n_heads,),
        in_specs=[pl.BlockSpec((s, HEAD_DIM), lambda h: (0, off + h))],
        out_specs=pl.BlockSpec((1, nb, HEAD_DIM), lambda h: (h, 0, 0)),
        out_shape=jax.ShapeDtypeStruct((n_heads, nb, HEAD_DIM), F32),
        compiler_params=_params("parallel"), name="moba_kmean")(proj)


_NT = (((1,), (1,)), ((), ()))
_TN = (((0,), (0,)), ((), ()))


def _attn_kernel(q_ref, k_ref, v_ref, km_ref, bias_ref, o_ref, neg_ref, acc_ref, *, group):
    i = pl.program_id(1)
    blk = MOBA_BLOCK
    gk = group * blk
    nb = neg_ref.shape[0]
    q = q_ref[...]

    km = km_ref[0]
    km_hi = km.astype(BF16)
    km_lo = (km - km_hi.astype(F32)).astype(BF16)
    gate = (lax.dot_general(km_hi, q, _NT, preferred_element_type=F32)
            + lax.dot_general(km_lo, q, _NT, preferred_element_type=F32))
    bidx = lax.broadcasted_iota(jnp.int32, (nb, blk), 0).astype(F32)
    own = i.astype(F32)
    past = bidx < own
    g = jnp.where(past, gate, NEG)
    chosen = jnp.zeros((nb, blk), jnp.bool_)
    for _ in range(MOBA_TOPK):
        mx = jnp.max(g, axis=0, keepdims=True)
        first = jnp.min(jnp.where(g == mx, bidx, float(nb)), axis=0, keepdims=True)
        pick = bidx == first
        chosen = jnp.logical_or(chosen, pick)
        g = jnp.where(pick, -jnp.inf, g)
    attend = jnp.logical_or(jnp.logical_and(chosen, past), bidx == own)
    neg_ref[...] = jnp.where(attend, 0.0, NEG)
    acc_ref[...] = jnp.zeros_like(acc_ref)

    def scores(gi):
        hk = gk // 2
        halves = []
        for part in range(2):
            kb = k_ref[pl.ds(pl.multiple_of(gi * gk + part * hk, hk), hk), :]
            halves.append(lax.dot_general(kb, q, _NT, preferred_element_type=F32))
        return jnp.concatenate(halves, axis=0)

    def update(s, gi, valid, m, acc):
        top = jnp.maximum(_strip_deltas(group)[0] - (i - gi * group), 0)
        s = s + bias_ref[0, pl.ds(pl.multiple_of(top * blk, blk), gk), :]
        pieces, rows = [], []
        m_new = m
        for b in range(group):
            row = neg_ref[pl.ds(gi * group + b, 1), :]
            if valid is not None:
                row = jnp.where(valid, row, NEG)
            piece = s[b * blk:(b + 1) * blk]
            m_new = jnp.maximum(m_new, jnp.max(piece, axis=0, keepdims=True) + row)
            pieces.append(piece)
            rows.append(row)
        probs = [jnp.exp2(piece - (m_new - row)).astype(BF16) for piece, row in zip(pieces, rows)]
        p = jnp.concatenate(probs, axis=0)
        alpha = jnp.exp2(m - m_new)
        hk = gk // 2
        ones = jnp.ones((hk, BF16_SUBLANES), BF16)
        pv, den = None, None
        for part in range(2):
            vb = v_ref[pl.ds(pl.multiple_of(gi * gk + part * hk, hk), hk), :]
            ph = p[part * hk:(part + 1) * hk]
            pv_h = lax.dot_general(vb, ph, _TN, preferred_element_type=F32)
            den_h = lax.dot_general(ones, ph, _TN, preferred_element_type=F32)
            pv = pv_h if pv is None else pv + pv_h
            den = den_h if den is None else den + den_h
        dh = vb.shape[1]
        acc[0:dh, :] = acc[0:dh, :] * alpha + pv
        acc[dh:, :] = acc[dh:, :] * alpha + den
        return m_new

    acc_a, acc_b = acc_ref.at[0], acc_ref.at[1]
    g_own = i // group
    m0 = update(scores(g_own), g_own, None, jnp.full((1, blk), NEG, F32), acc_a)

    def pair(t, carry):
        m_a, m_b = carry
        g_a = g_own - 1 - 2 * t
        g_b = jnp.maximum(g_a - 1, 0)
        s_a = scores(g_a)
        s_b = scores(g_b)
        m_a = update(s_a, g_a, None, m_a, acc_a)
        m_b = update(s_b, g_b, g_a >= 1, m_b, acc_b)
        return m_a, m_b

    m_a, m_b = lax.fori_loop(0, (g_own + 1) // 2, pair, (m0, m0))
    m = jnp.maximum(m_a, m_b)
    merged = acc_a[...] * jnp.exp2(m_a - m) + acc_b[...] * jnp.exp2(m_b - m)
    dh = o_ref.shape[1]
    out = merged[0:dh] * (1.0 / merged[dh:dh + 1])
    o_ref[...] = out.T.astype(o_ref.dtype)


def _key_group(seq):
    nb = seq // MOBA_BLOCK
    return KEY_GROUP if nb % KEY_GROUP == 0 else 1


def _moba(proj, col_q, col_k, col_v, n_heads, bias):
    s = proj.shape[0]
    blk, dh = MOBA_BLOCK, HEAD_DIM
    nb = s // blk
    group = _key_group(s)
    km = _kmean(proj, col_k, n_heads)
    vrows = dh + BF16_SUBLANES
    q_off, k_off, v_off = col_q // dh, col_k // dh, col_v // dh
    return pl.pallas_call(
        functools.partial(_attn_kernel, group=group), grid=(n_heads, nb),
        in_specs=[pl.BlockSpec((blk, dh), lambda h, i: (i, q_off + h)),
                  pl.BlockSpec((s, dh), lambda h, i: (0, k_off + h)),
                  pl.BlockSpec((s, dh), lambda h, i: (0, v_off + h)),
                  pl.BlockSpec((1, nb, dh), lambda h, i: (h, 0, 0)),
                  pl.BlockSpec((1, bias.shape[1], blk), lambda h, i: (h, 0, 0))],
        out_specs=pl.BlockSpec((blk, dh), lambda h, i: (i, h)),
        out_shape=jax.ShapeDtypeStruct((s, n_heads * dh), BF16),
        scratch_shapes=[pltpu.VMEM((nb, blk), F32), pltpu.VMEM((2, vrows, blk), F32)],
        compiler_params=_params("parallel", "arbitrary"), name="moba_attention")(
            proj, proj, proj, km, bias)


def _layer(x, h, bias_strip, g_post_mix, g_pre_ffn, g_post_ffn, g_next, w_in, conv_a_w, conv_a_b,
           lru_wa, lru_ba, lru_wx, lru_bx, lru_lambda, conv_c_w, w_branch, w_gate, w_out,
           w_ffn_in, w_ffn_out):
    s, d = x.shape
    lru_w = conv_a_w.shape[1]
    sc_w = conv_c_w.shape[1]
    att_w = (w_in.shape[1] - 2 * lru_w - 3 * sc_w) // 3
    n_heads = att_w // HEAD_DIM
    col_ga = lru_w
    col_q = 2 * lru_w
    col_k = col_q + att_w
    col_v = col_k + att_w
    col_sb = col_v + att_w
    col_sc = col_sb + sc_w
    col_sx = col_sc + sc_w

    bm = _pick(s, (1024, 512, 256))
    col = jnp.arange(w_in.shape[1])
    col_scale = jnp.where((col >= col_q) & (col < col_k), QK_SCALE, 1.0).astype(F32)
    proj = _matmul_colscale(h, w_in.astype(BF16), col_scale, BF16, bm=bm,
                            bn=_pick(w_in.shape[1], (768, 512, 256, 128)), name="in_proj")
    gates = _matmul(h, w_gate.astype(BF16), BF16, bm=bm, bn=_pick(w_gate.shape[1], (768, 512, 256, 128)),
                    sigmoid=True, name="gate_proj")

    ya = _mix_a(proj, 0, col_ga, conv_a_w, conv_a_b, lru_wa, lru_ba, lru_wx, lru_bx, lru_lambda)
    yb = _moba(proj, col_q, col_k, col_v, n_heads, bias_strip)
    yc = _mix_c(proj, col_sb, col_sc, col_sx, conv_c_w)

    merged = _merge(ya, yb, yc, w_branch.astype(BF16), gates, bm=bm, bn=_pick(d, (512, 256, 128)))
    mix = _matmul(merged, w_out.astype(BF16), BF16, bm=bm, bn=_pick(d, (512, 256, 128)), name="out_proj")
    x, h2 = _residual_norm(x, mix, g_post_mix, g_pre_ffn)

    act = _ffn_in(h2, w_ffn_in.astype(BF16), bm=_pick(s, (2048, 1024, 512, 256)),
                  bn=_pick(w_ffn_in.shape[1] // 2, (256, 128)))
    f = _matmul(act, w_ffn_out.astype(BF16), BF16, bm=_pick(s, (512, 256)), bn=_pick(d, (512, 256, 128)),
                name="ffn_out")
    return _residual_norm(x, f, g_post_ffn, g_next)


@jax.jit
def _forward(x, rel_bias, norm_pre_mix, norm_post_mix, norm_pre_ffn, norm_post_ffn, w_in, conv_a_w,
             conv_a_b, lru_wa, lru_ba, lru_wx, lru_bx, lru_lambda, conv_c_w, w_branch, w_gate, w_out,
             w_ffn_in, w_ffn_out):
    bsz, s, d = x.shape
    depth = w_in.shape[0]
    bias_strip = _bias_strip(rel_bias, _key_group(s))
    outs = []
    for b in range(bsz):
        xb = x[b]
        h = _norm(xb, norm_pre_mix[0])
        for l in range(depth):
            g_next = norm_pre_mix[l + 1] if l + 1 < depth else None
            xb, h = _layer(xb, h, bias_strip, norm_post_mix[l], norm_pre_ffn[l], norm_post_ffn[l], g_next,
                           w_in[l], conv_a_w[l], conv_a_b[l], lru_wa[l], lru_ba[l], lru_wx[l], lru_bx[l],
                           lru_lambda[l], conv_c_w[l], w_branch[l], w_gate[l], w_out[l], w_ffn_in[l],
                           w_ffn_out[l])
        outs.append(xb)
    return jnp.stack(outs)


def kernel(x, rel_bias, norm_pre_mix, norm_post_mix, norm_pre_ffn, norm_post_ffn, w_in, conv_a_w, conv_a_b, lru_wa, lru_ba, lru_wx, lru_bx, lru_lambda, conv_c_w, w_branch, w_gate, w_out, w_ffn_in, w_ffn_out):
    return _forward(x, rel_bias, norm_pre_mix, norm_post_mix, norm_pre_ffn, norm_post_ffn, w_in, conv_a_w,
                    conv_a_b, lru_wa, lru_ba, lru_wx, lru_bx, lru_lambda, conv_c_w, w_branch, w_gate, w_out,
                    w_ffn_in, w_ffn_out)
```

```python
import functools
import math

import numpy as np
import jax
import jax.numpy as jnp
from jax import lax
from jax.experimental import pallas as pl
from jax.experimental.pallas import tpu as pltpu

F32 = jnp.float32
BF16 = jnp.bfloat16

EPS = 1e-6
NEG = -1e30
LOG2E = 1.4426950408889634

HEAD_DIM = 128
QK_SCALE = HEAD_DIM ** -0.5 * LOG2E
MOBA_BLOCK = 256
MOBA_TOPK = 3
REL_BUCKETS = 32
REL_MAX_DIST = 2048
LRU_BLOCK_DIM = 128
LRU_C = 8.0
N_BRANCH = 3
CHAN_TILE = 512
TIME_TILE = 256
CONV_HALO = 8
BF16_SUBLANES = 16
VMEM_LIMIT = 56 * 1024 * 1024
CAST_BLOCK_BYTES = 4 * 1024 * 1024


def _pick(n, candidates):
    for c in candidates:
        if n % c == 0:
            return c
    raise ValueError(f"no tile in {candidates} divides {n}")


def _params(*sem):
    return pltpu.CompilerParams(dimension_semantics=sem, vmem_limit_bytes=VMEM_LIMIT)


def _rms(x, g):
    return x * lax.rsqrt(jnp.mean(x * x, axis=-1, keepdims=True) + EPS) * g


def _norm_kernel(x_ref, g_ref, h_ref):
    h_ref[...] = _rms(x_ref[...], g_ref[...]).astype(h_ref.dtype)


def _resnorm_kernel(x_ref, y_ref, gp_ref, gn_ref, xo_ref, h_ref):
    xn = x_ref[...] + _rms(y_ref[...].astype(F32), gp_ref[...])
    xo_ref[...] = xn
    h_ref[...] = _rms(xn, gn_ref[...]).astype(h_ref.dtype)


def _res_kernel(x_ref, y_ref, gp_ref, xo_ref):
    xo_ref[...] = x_ref[...] + _rms(y_ref[...].astype(F32), gp_ref[...])


def _norm(x, g):
    s, d = x.shape
    bm = _pick(s, (256, 128, 8))
    row = pl.BlockSpec((bm, d), lambda i: (i, 0))
    vec = pl.BlockSpec((1, d), lambda i: (0, 0))
    return pl.pallas_call(
        _norm_kernel, grid=(s // bm,), in_specs=[row, vec], out_specs=row,
        out_shape=jax.ShapeDtypeStruct((s, d), BF16),
        compiler_params=_params("parallel"), name="rmsnorm")(x, g.reshape(1, d))


def _residual_norm(x, y, g_post, g_next):
    s, d = x.shape
    bm = _pick(s, (256, 128, 8))
    row = pl.BlockSpec((bm, d), lambda i: (i, 0))
    vec = pl.BlockSpec((1, d), lambda i: (0, 0))
    if g_next is None:
        return pl.pallas_call(
            _res_kernel, grid=(s // bm,), in_specs=[row, row, vec], out_specs=row,
            out_shape=jax.ShapeDtypeStruct((s, d), F32),
            compiler_params=_params("parallel"), name="residual")(x, y, g_post.reshape(1, d)), None
    return pl.pallas_call(
        _resnorm_kernel, grid=(s // bm,), in_specs=[row, row, vec, vec], out_specs=[row, row],
        out_shape=[jax.ShapeDtypeStruct((s, d), F32), jax.ShapeDtypeStruct((s, d), BF16)],
        compiler_params=_params("parallel"), name="residual_norm")(
            x, y, g_post.reshape(1, d), g_next.reshape(1, d))


def _mm_kernel(a_ref, w_ref, o_ref, *, sigmoid):
    acc = jnp.dot(a_ref[...], w_ref[...], preferred_element_type=F32)
    if sigmoid:
        acc = jax.nn.sigmoid(acc)
    o_ref[...] = acc.astype(o_ref.dtype)


def _mm_colscale_kernel(a_ref, w_ref, s_ref, o_ref):
    acc = jnp.dot(a_ref[...], w_ref[...], preferred_element_type=F32)
    o_ref[...] = (acc * s_ref[...]).astype(o_ref.dtype)


def _cast_kernel(w_ref, o_ref):
    o_ref[...] = w_ref[...].astype(o_ref.dtype)


def _to_bf16(w):
    depth, k, n = w.shape
    bk = _pick(k, [c for c in (512, 256, 128, 64, 32, 16) if c * n * 4 <= CAST_BLOCK_BYTES])
    spec = pl.BlockSpec((1, bk, n), lambda l, i: (l, i, 0))
    return pl.pallas_call(
        _cast_kernel, grid=(depth, k // bk), in_specs=[spec], out_specs=spec,
        out_shape=jax.ShapeDtypeStruct(w.shape, BF16),
        compiler_params=_params("parallel", "parallel"), name="weight_cast")(w)


def _matmul_colscale(a, w, layer, col_scale, out_dtype, *, bm, bn, name):
    m, k = a.shape
    n = w.shape[2]
    return pl.pallas_call(
        _mm_colscale_kernel,
        grid=(m // bm, n // bn),
        in_specs=[pl.BlockSpec((bm, k), lambda i, j: (i, 0)),
                  pl.BlockSpec((None, k, bn), lambda i, j: (layer, 0, j)),
                  pl.BlockSpec((1, bn), lambda i, j: (0, j))],
        out_specs=pl.BlockSpec((bm, bn), lambda i, j: (i, j)),
        out_shape=jax.ShapeDtypeStruct((m, n), out_dtype),
        compiler_params=_params("parallel", "arbitrary"), name=name)(a, w, col_scale.reshape(1, n))


def _matmul(a, w, layer, out_dtype, *, bm, bn, sigmoid=False, name):
    m, k = a.shape
    n = w.shape[2]
    return pl.pallas_call(
        functools.partial(_mm_kernel, sigmoid=sigmoid),
        grid=(m // bm, n // bn),
        in_specs=[pl.BlockSpec((bm, k), lambda i, j: (i, 0)),
                  pl.BlockSpec((None, k, bn), lambda i, j: (layer, 0, j))],
        out_specs=pl.BlockSpec((bm, bn), lambda i, j: (i, j)),
        out_shape=jax.ShapeDtypeStruct((m, n), out_dtype),
        compiler_params=_params("parallel", "arbitrary"), name=name)(a, w)


def _merge_kernel(ya_ref, yb_ref, yc_ref, wa_ref, wb_ref, wc_ref, ga_ref, gb_ref, gc_ref, o_ref):
    pa = jnp.dot(ya_ref[...], wa_ref[...], preferred_element_type=F32)
    pb = jnp.dot(yb_ref[...], wb_ref[...], preferred_element_type=F32)
    pc = jnp.dot(yc_ref[...], wc_ref[...], preferred_element_type=F32)
    merged = (ga_ref[...].astype(F32) * pa + gb_ref[...].astype(F32) * pb
              + gc_ref[...].astype(F32) * pc)
    o_ref[...] = merged.astype(o_ref.dtype)


def _merge(ya, yb, yc, w, layer, gates, *, bm, bn):
    s = ya.shape[0]
    d = w.shape[2]
    nj = d // bn
    ka, kb, kc = ya.shape[1], yb.shape[1], yc.shape[1]
    assert ka % kb == 0 and (ka + kb) % kc == 0, "branch widths must tile the rows of w_branch"

    def act(k):
        return pl.BlockSpec((bm, k), lambda i, j: (i, 0))

    def wgt(k, row_off):
        return pl.BlockSpec((None, k, bn), lambda i, j: (layer, row_off // k, j))

    def gate(b):
        return pl.BlockSpec((bm, bn), lambda i, j: (i, b * nj + j))

    return pl.pallas_call(
        _merge_kernel, grid=(s // bm, nj),
        in_specs=[act(ka), act(kb), act(kc), wgt(ka, 0), wgt(kb, ka), wgt(kc, ka + kb),
                  gate(0), gate(1), gate(2)],
        out_specs=pl.BlockSpec((bm, bn), lambda i, j: (i, j)),
        out_shape=jax.ShapeDtypeStruct((s, d), BF16),
        compiler_params=_params("parallel", "arbitrary"), name="branch_merge")(
            ya, yb, yc, w, w, w, gates, gates, gates)


def _ffn_in_kernel(h_ref, wg_ref, wu_ref, o_ref):
    h = h_ref[...]
    g = jnp.dot(h, wg_ref[...], preferred_element_type=F32)
    u = jnp.dot(h, wu_ref[...], preferred_element_type=F32)
    o_ref[...] = (g * jax.nn.sigmoid(g) * u).astype(o_ref.dtype)


def _ffn_in(h, w, layer, *, bm, bn):
    s, d = h.shape
    d_ff = w.shape[2] // 2
    nj = d_ff // bn
    return pl.pallas_call(
        _ffn_in_kernel, grid=(s // bm, nj),
        in_specs=[pl.BlockSpec((bm, d), lambda i, j: (i, 0)),
                  pl.BlockSpec((None, d, bn), lambda i, j: (layer, 0, j)),
                  pl.BlockSpec((None, d, bn), lambda i, j: (layer, 0, nj + j))],
        out_specs=pl.BlockSpec((bm, bn), lambda i, j: (i, j)),
        out_shape=jax.ShapeDtypeStruct((s, d_ff), BF16),
        compiler_params=_params("parallel", "arbitrary"), name="ffn_in")(h, w, w)


def _causal_conv(x, w_ref, xbuf, width):
    t = x.shape[0]
    xbuf[CONV_HALO:CONV_HALO + t, :] = x
    out = None
    for k in range(width):
        off = CONV_HALO - (width - 1) + k
        term = w_ref[k:k + 1, :] * xbuf[off:off + t, :]
        out = term if out is None else out + term
    xbuf[0:CONV_HALO, :] = xbuf[t:t + CONV_HALO, :]
    return out


def _linear_scan(a, u):
    t = a.shape[0]
    row = lax.broadcasted_iota(jnp.int32, a.shape, 0)
    s = 1
    while s < t:
        keep = row >= s
        a_prev = pltpu.roll(a, s, 0)
        u_prev = pltpu.roll(u, s, 0)
        u = jnp.where(keep, a * u_prev + u, u)
        a = jnp.where(keep, a * a_prev, a)
        s *= 2
    return a, u


def _mix_a_kernel(xa_ref, ga_ref, cw_ref, cb_ref, wa_ref, ba_ref, wx_ref, bx_ref, lam_ref,
                  o_ref, xbuf, hcar):
    @pl.when(pl.program_id(1) == 0)
    def _():
        xbuf[0:CONV_HALO, :] = jnp.zeros((CONV_HALO, xbuf.shape[1]), F32)
        hcar[...] = jnp.zeros_like(hcar)

    t, c = xa_ref.shape
    x = _causal_conv(xa_ref[...].astype(F32), cw_ref, xbuf, cw_ref.shape[0]) + cb_ref[...]
    xb = x.astype(BF16)
    r_parts, i_parts = [], []
    for g in range(c // LRU_BLOCK_DIM):
        xg = xb[:, g * LRU_BLOCK_DIM:(g + 1) * LRU_BLOCK_DIM]
        r_parts.append(jnp.dot(xg, wa_ref[g], preferred_element_type=F32))
        i_parts.append(jnp.dot(xg, wx_ref[g], preferred_element_type=F32))
    r = jax.nn.sigmoid(jnp.concatenate(r_parts, axis=1) + ba_ref[...])
    i = jax.nn.sigmoid(jnp.concatenate(i_parts, axis=1) + bx_ref[...])
    z = -lam_ref[...]
    softplus = jnp.maximum(z, 0.0) + jnp.log1p(jnp.exp(-jnp.abs(z)))
    log_a = (-LRU_C) * r * softplus
    a = jnp.exp(log_a)
    th = jnp.tanh(log_a)
    u = jnp.sqrt(-2.0 * th / (1.0 - th)) * (i * x)
    a_cum, h = _linear_scan(a, u)
    h = h + a_cum * hcar[...]
    hcar[...] = h[t - 1:t, :]
    o_ref[...] = (jax.nn.gelu(ga_ref[...].astype(F32)) * h).astype(o_ref.dtype)


def _mix_a(proj, col_x, col_g, conv_w, conv_b, w_a, b_a, w_x, b_x, lam):
    s = proj.shape[0]
    width = conv_w.shape[1]
    tc = CHAN_TILE
    tt = _pick(s, (TIME_TILE,))
    gpt = tc // LRU_BLOCK_DIM
    x_off, g_off = col_x // tc, col_g // tc
    vec = pl.BlockSpec((1, tc), lambda c, t: (0, c))
    blk = pl.BlockSpec((gpt, LRU_BLOCK_DIM, LRU_BLOCK_DIM), lambda c, t: (c, 0, 0))
    return pl.pallas_call(
        _mix_a_kernel, grid=(width // tc, s // tt),
        in_specs=[pl.BlockSpec((tt, tc), lambda c, t: (t, x_off + c)),
                  pl.BlockSpec((tt, tc), lambda c, t: (t, g_off + c)),
                  pl.BlockSpec((conv_w.shape[0], tc), lambda c, t: (0, c)),
                  vec, blk, vec, blk, vec, vec],
        out_specs=pl.BlockSpec((tt, tc), lambda c, t: (t, c)),
        out_shape=jax.ShapeDtypeStruct((s, width), BF16),
        scratch_shapes=[pltpu.VMEM((tt + CONV_HALO, tc), F32), pltpu.VMEM((1, tc), F32)],
        compiler_params=_params("arbitrary", "arbitrary"), name="mixer_rglru")(
            proj, proj, conv_w, conv_b.reshape(1, width), w_a.astype(BF16), b_a.reshape(1, width),
            w_x.astype(BF16), b_x.reshape(1, width), lam.reshape(1, width))


def _mix_c_kernel(sb_ref, sc_ref, sx_ref, cw_ref, o_ref, xbuf):
    @pl.when(pl.program_id(1) == 0)
    def _():
        xbuf[0:CONV_HALO, :] = jnp.zeros((CONV_HALO, xbuf.shape[1]), F32)

    z = sc_ref[...].astype(F32) * sx_ref[...].astype(F32)
    conv = _causal_conv(z, cw_ref, xbuf, cw_ref.shape[0])
    o_ref[...] = (sb_ref[...].astype(F32) * conv).astype(o_ref.dtype)


def _mix_c(proj, col_b, col_c, col_x, conv_w):
    s = proj.shape[0]
    width = conv_w.shape[1]
    tc = CHAN_TILE
    tt = _pick(s, (TIME_TILE,))

    def col(off):
        return pl.BlockSpec((tt, tc), lambda c, t: (t, off // tc + c))

    return pl.pallas_call(
        _mix_c_kernel, grid=(width // tc, s // tt),
        in_specs=[col(col_b), col(col_c), col(col_x),
                  pl.BlockSpec((conv_w.shape[0], tc), lambda c, t: (0, c))],
        out_specs=pl.BlockSpec((tt, tc), lambda c, t: (t, c)),
        out_shape=jax.ShapeDtypeStruct((s, width), BF16),
        scratch_shapes=[pltpu.VMEM((tt + CONV_HALO, tc), F32)],
        compiler_params=_params("arbitrary", "arbitrary"), name="mixer_shortconv")(
            proj, proj, proj, conv_w)


def _t5_bucket_np(dist):
    n = np.maximum(dist, 0)
    max_exact = REL_BUCKETS // 2
    nf = np.maximum(n, 1).astype(np.float32)
    large = max_exact + (np.log(nf / np.float32(max_exact)) / np.float32(math.log(REL_MAX_DIST / max_exact))
                         * np.float32(REL_BUCKETS - max_exact)).astype(np.int32)
    large = np.minimum(large, REL_BUCKETS - 1)
    return np.where(n < max_exact, n, large).astype(np.int32)


def _near_offsets():
    d = 0
    while True:
        lo = d * MOBA_BLOCK - (MOBA_BLOCK - 1)
        if lo > 0 and _t5_bucket_np(np.array([lo]))[0] == REL_BUCKETS - 1:
            return d
        d += 1


N_NEAR = _near_offsets()


KEY_GROUP = 4


def _strip_deltas(group):
    return list(range(N_NEAR - 1 + 2 * group - 1, -group, -1))


def _bucket_strip(group):
    kr = np.arange(MOBA_BLOCK)[:, None]
    qc = np.arange(MOBA_BLOCK)[None, :]
    tiles = []
    for d in _strip_deltas(group):
        rel = d * MOBA_BLOCK + qc - kr
        tiles.append(np.where(rel >= 0, _t5_bucket_np(rel), REL_BUCKETS))
    return np.stack(tiles).astype(np.int32)


def _bias_kernel(tbl_ref, bucket_ref, o_ref):
    h = pl.program_id(0)
    bucket = bucket_ref[0]
    acc = jnp.full(bucket.shape, NEG, F32)
    for b in range(REL_BUCKETS):
        acc = jnp.where(bucket == b, tbl_ref[h * REL_BUCKETS + b], acc)
    o_ref[0] = acc


def _bias_strip(rel_bias, group):
    n_heads = rel_bias.shape[1]
    tbl = (rel_bias.astype(F32).T * LOG2E).reshape(-1)
    buckets = jnp.asarray(_bucket_strip(group))
    n_strip = buckets.shape[0]
    blk = MOBA_BLOCK
    return pl.pallas_call(
        _bias_kernel, grid=(n_heads, n_strip),
        in_specs=[pl.BlockSpec(memory_space=pltpu.SMEM),
                  pl.BlockSpec((1, blk, blk), lambda h, d: (d, 0, 0))],
        out_specs=pl.BlockSpec((1, blk, blk), lambda h, d: (h, d, 0)),
        out_shape=jax.ShapeDtypeStruct((n_heads, n_strip * blk, blk), F32),
        compiler_params=_params("arbitrary", "arbitrary"), name="t5_bias_strip")(tbl, buckets)


def _kmean_kernel(k_ref, o_ref):
    k = k_ref[...].astype(F32)
    nb = k.shape[0] // MOBA_BLOCK
    o_ref[0] = jnp.mean(k.reshape(nb, MOBA_BLOCK, k.shape[1]), axis=1)


def _kmean(proj, col_k, n_heads):
    s = proj.shape[0]
    nb = s // MOBA_BLOCK
    off = col_k // HEAD_DIM
    return pl.pallas_call(
        _kmean_kernel, grid=(n_heads,),
        in_specs=[pl.BlockSpec((s, HEAD_DIM), lambda h: (0, off + h))],
        out_specs=pl.BlockSpec((1, nb, HEAD_DIM), lambda h: (h, 0, 0)),
        out_shape=jax.ShapeDtypeStruct((n_heads, nb, HEAD_DIM), F32),
        compiler_params=_params("parallel"), name="moba_kmean")(proj)


_NT = (((1,), (1,)), ((), ()))
_TN = (((0,), (0,)), ((), ()))


def _attn_kernel(q_ref, k_ref, v_ref, km_ref, bias_ref, o_ref, neg_ref, acc_ref, *, group):
    i = pl.program_id(1)
    blk = MOBA_BLOCK
    gk = group * blk
    nb = neg_ref.shape[0]
    q = q_ref[...]

    km = km_ref[0]
    km_hi = km.astype(BF16)
    km_lo = (km - km_hi.astype(F32)).astype(BF16)
    gate = (lax.dot_general(km_hi, q, _NT, preferred_element_type=F32)
            + lax.dot_general(km_lo, q, _NT, preferred_element_type=F32))
    bidx = lax.broadcasted_iota(jnp.int32, (nb, blk), 0).astype(F32)
    own = i.astype(F32)
    past = bidx < own
    g = jnp.where(past, gate, NEG)
    chosen = jnp.zeros((nb, blk), jnp.bool_)
    for _ in range(MOBA_TOPK):
        mx = jnp.max(g, axis=0, keepdims=True)
        first = jnp.min(jnp.where(g == mx, bidx, float(nb)), axis=0, keepdims=True)
        pick = bidx == first
        chosen = jnp.logical_or(chosen, pick)
        g = jnp.where(pick, -jnp.inf, g)
    attend = jnp.logical_or(jnp.logical_and(chosen, past), bidx == own)
    neg_ref[...] = jnp.where(attend, 0.0, NEG)
    acc_ref[...] = jnp.zeros_like(acc_ref)

    def scores(gi):
        hk = gk // 2
        halves = []
        for part in range(2):
            kb = k_ref[pl.ds(pl.multiple_of(gi * gk + part * hk, hk), hk), :]
            halves.append(lax.dot_general(kb, q, _NT, preferred_element_type=F32))
        return jnp.concatenate(halves, axis=0)

    def update(s, gi, valid, m, acc):
        top = jnp.maximum(_strip_deltas(group)[0] - (i - gi * group), 0)
        s = s + bias_ref[0, pl.ds(pl.multiple_of(top * blk, blk), gk), :]
        pieces, rows = [], []
        m_new = m
        for b in range(group):
            row = neg_ref[pl.ds(gi * group + b, 1), :]
            if valid is not None:
                row = jnp.where(valid, row, NEG)
            piece = s[b * blk:(b + 1) * blk]
            m_new = jnp.maximum(m_new, jnp.max(piece, axis=0, keepdims=True) + row)
            pieces.append(piece)
            rows.append(row)
        probs = [jnp.exp2(piece - (m_new - row)).astype(BF16) for piece, row in zip(pieces, rows)]
        p = jnp.concatenate(probs, axis=0)
        alpha = jnp.exp2(m - m_new)
        hk = gk // 2
        ones = jnp.ones((hk, BF16_SUBLANES), BF16)
        pv, den = None, None
        for part in range(2):
            vb = v_ref[pl.ds(pl.multiple_of(gi * gk + part * hk, hk), hk), :]
            ph = p[part * hk:(part + 1) * hk]
            pv_h = lax.dot_general(vb, ph, _TN, preferred_element_type=F32)
            den_h = lax.dot_general(ones, ph, _TN, preferred_element_type=F32)
            pv = pv_h if pv is None else pv + pv_h
            den = den_h if den is None else den + den_h
        dh = vb.shape[1]
        acc[0:dh, :] = acc[0:dh, :] * alpha + pv
        acc[dh:, :] = acc[dh:, :] * alpha + den
        return m_new

    acc_a, acc_b = acc_ref.at[0], acc_ref.at[1]
    g_own = i // group
    m0 = update(scores(g_own), g_own, None, jnp.full((1, blk), NEG, F32), acc_a)

    def pair(t, carry):
        m_a, m_b = carry
        g_a = g_own - 1 - 2 * t
        g_b = jnp.maximum(g_a - 1, 0)
        s_a = scores(g_a)
        s_b = scores(g_b)
        m_a = update(s_a, g_a, None, m_a, acc_a)
        m_b = update(s_b, g_b, g_a >= 1, m_b, acc_b)
        return m_a, m_b

    m_a, m_b = lax.fori_loop(0, (g_own + 1) // 2, pair, (m0, m0))
    m = jnp.maximum(m_a, m_b)
    merged = acc_a[...] * jnp.exp2(m_a - m) + acc_b[...] * jnp.exp2(m_b - m)
    dh = o_ref.shape[1]
    out = merged[0:dh] * (1.0 / merged[dh:dh + 1])
    o_ref[...] = out.T.astype(o_ref.dtype)


def _key_group(seq):
    nb = seq // MOBA_BLOCK
    return KEY_GROUP if nb % KEY_GROUP == 0 else 1


def _moba(proj, col_q, col_k, col_v, n_heads, bias):
    s = proj.shape[0]
    blk, dh = MOBA_BLOCK, HEAD_DIM
    nb = s // blk
    group = _key_group(s)
    km = _kmean(proj, col_k, n_heads)
    vrows = dh + BF16_SUBLANES
    q_off, k_off, v_off = col_q // dh, col_k // dh, col_v // dh
    return pl.pallas_call(
        functools.partial(_attn_kernel, group=group), grid=(n_heads, nb),
        in_specs=[pl.BlockSpec((blk, dh), lambda h, i: (i, q_off + h)),
                  pl.BlockSpec((s, dh), lambda h, i: (0, k_off + h)),
                  pl.BlockSpec((s, dh), lambda h, i: (0, v_off + h)),
                  pl.BlockSpec((1, nb, dh), lambda h, i: (h, 0, 0)),
                  pl.BlockSpec((1, bias.shape[1], blk), lambda h, i: (h, 0, 0))],
        out_specs=pl.BlockSpec((blk, dh), lambda h, i: (i, h)),
        out_shape=jax.ShapeDtypeStruct((s, n_heads * dh), BF16),
        scratch_shapes=[pltpu.VMEM((nb, blk), F32), pltpu.VMEM((2, vrows, blk), F32)],
        compiler_params=_params("parallel", "arbitrary"), name="moba_attention")(
            proj, proj, proj, km, bias)


def _layer(x, h, layer, bias_strip, g_post_mix, g_pre_ffn, g_post_ffn, g_next, w_in, conv_a_w, conv_a_b,
           lru_wa, lru_ba, lru_wx, lru_bx, lru_lambda, conv_c_w, w_branch, w_gate, w_out,
           w_ffn_in, w_ffn_out):
    s, d = x.shape
    in_w = w_in.shape[2]
    lru_w = conv_a_w.shape[1]
    sc_w = conv_c_w.shape[1]
    att_w = (in_w - 2 * lru_w - 3 * sc_w) // 3
    n_heads = att_w // HEAD_DIM
    col_ga = lru_w
    col_q = 2 * lru_w
    col_k = col_q + att_w
    col_v = col_k + att_w
    col_sb = col_v + att_w
    col_sc = col_sb + sc_w
    col_sx = col_sc + sc_w

    bm = _pick(s, (1024, 512, 256))
    col = jnp.arange(in_w)
    col_scale = jnp.where((col >= col_q) & (col < col_k), QK_SCALE, 1.0).astype(F32)
    proj = _matmul_colscale(h, w_in, layer, col_scale, BF16, bm=bm,
                            bn=_pick(in_w, (768, 512, 256, 128)), name="in_proj")
    gates = _matmul(h, w_gate, layer, BF16, bm=bm, bn=_pick(w_gate.shape[2], (1024, 512, 256, 128)),
                    sigmoid=True, name="gate_proj")

    ya = _mix_a(proj, 0, col_ga, conv_a_w, conv_a_b, lru_wa, lru_ba, lru_wx, lru_bx, lru_lambda)
    yb = _moba(proj, col_q, col_k, col_v, n_heads, bias_strip)
    yc = _mix_c(proj, col_sb, col_sc, col_sx, conv_c_w)

    merged = _merge(ya, yb, yc, w_branch, layer, gates, bm=bm, bn=_pick(d, (512, 256, 128)))
    mix = _matmul(merged, w_out, layer, BF16, bm=bm, bn=_pick(d, (512, 256, 128)), name="out_proj")
    x, h2 = _residual_norm(x, mix, g_post_mix, g_pre_ffn)

    act = _ffn_in(h2, w_ffn_in, layer, bm=_pick(s, (2048, 1024, 512, 256)),
                  bn=_pick(w_ffn_in.shape[2] // 2, (256, 128)))
    f = _matmul(act, w_ffn_out, layer, BF16, bm=_pick(s, (512, 256)), bn=_pick(d, (512, 256, 128)),
                name="ffn_out")
    return _residual_norm(x, f, g_post_ffn, g_next)


@jax.jit
def _forward(x, rel_bias, norm_pre_mix, norm_post_mix, norm_pre_ffn, norm_post_ffn, w_in, conv_a_w,
             conv_a_b, lru_wa, lru_ba, lru_wx, lru_bx, lru_lambda, conv_c_w, w_branch, w_gate, w_out,
             w_ffn_in, w_ffn_out):
    bsz, s, d = x.shape
    depth = w_in.shape[0]
    bias_strip = _bias_strip(rel_bias, _key_group(s))
    w_in, w_branch, w_gate, w_out, w_ffn_in, w_ffn_out = (
        _to_bf16(w) for w in (w_in, w_branch, w_gate, w_out, w_ffn_in, w_ffn_out))
    outs = []
    for b in range(bsz):
        xb = x.reshape(s, d) if bsz == 1 else x[b]
        h = _norm(xb, norm_pre_mix[0])
        for l in range(depth):
            g_next = norm_pre_mix[l + 1] if l + 1 < depth else None
            xb, h = _layer(xb, h, l, bias_strip, norm_post_mix[l], norm_pre_ffn[l], norm_post_ffn[l], g_next,
                           w_in, conv_a_w[l], conv_a_b[l], lru_wa[l], lru_ba[l], lru_wx[l], lru_bx[l],
                           lru_lambda[l], conv_c_w[l], w_branch, w_gate, w_out, w_ffn_in, w_ffn_out)
        outs.append(xb)
    return outs[0].reshape(1, s, d) if bsz == 1 else jnp.stack(outs)


def kernel(x, rel_bias, norm_pre_mix, norm_post_mix, norm_pre_ffn, norm_post_ffn, w_in, conv_a_w, conv_a_b, lru_wa, lru_ba, lru_wx, lru_bx, lru_lambda, conv_c_w, w_branch, w_gate, w_out, w_ffn_in, w_ffn_out):
    return _forward(x, rel_bias, norm_pre_mix, norm_post_mix, norm_pre_ffn, norm_post_ffn, w_in, conv_a_w,
                    conv_a_b, lru_wa, lru_ba, lru_wx, lru_bx, lru_lambda, conv_c_w, w_branch, w_gate, w_out,
                    w_ffn_in, w_ffn_out)
```

```python
import functools
import math

import numpy as np
import jax
import jax.numpy as jnp
from jax import lax
from jax.experimental import pallas as pl
from jax.experimental.pallas import tpu as pltpu

F32 = jnp.float32
BF16 = jnp.bfloat16

EPS = 1e-6
NEG = -1e30
LOG2E = 1.4426950408889634

HEAD_DIM = 128
QK_SCALE = HEAD_DIM ** -0.5 * LOG2E
MOBA_BLOCK = 256
MOBA_TOPK = 3
REL_BUCKETS = 32
REL_MAX_DIST = 2048
LRU_BLOCK_DIM = 128
LRU_C = 8.0
N_BRANCH = 3
CHAN_TILE = 512
TIME_TILE = 256
CONV_HALO = 8
BF16_SUBLANES = 16
F32_SUBLANES = 8
F32_TINY = float(np.finfo(np.float32).tiny)
VMEM_LIMIT = 56 * 1024 * 1024
CAST_BLOCK_BYTES = 4 * 1024 * 1024


def _pick(n, candidates):
    for c in candidates:
        if n % c == 0:
            return c
    raise ValueError(f"no tile in {candidates} divides {n}")


def _params(*sem):
    return pltpu.CompilerParams(dimension_semantics=sem, vmem_limit_bytes=VMEM_LIMIT)


def _rms(x, g):
    return x * lax.rsqrt(jnp.mean(x * x, axis=-1, keepdims=True) + EPS) * g


def _norm_kernel(x_ref, g_ref, h_ref):
    h_ref[...] = _rms(x_ref[...], g_ref[...]).astype(h_ref.dtype)


def _resnorm_kernel(x_ref, y_ref, gp_ref, gn_ref, xo_ref, h_ref):
    xn = x_ref[...] + _rms(y_ref[...].astype(F32), gp_ref[...])
    xo_ref[...] = xn
    h_ref[...] = _rms(xn, gn_ref[...]).astype(h_ref.dtype)


def _res_kernel(x_ref, y_ref, gp_ref, xo_ref):
    xo_ref[...] = x_ref[...] + _rms(y_ref[...].astype(F32), gp_ref[...])


def _norm(x, g):
    s, d = x.shape
    bm = _pick(s, (256, 128, 8))
    row = pl.BlockSpec((bm, d), lambda i: (i, 0))
    vec = pl.BlockSpec((1, d), lambda i: (0, 0))
    return pl.pallas_call(
        _norm_kernel, grid=(s // bm,), in_specs=[row, vec], out_specs=row,
        out_shape=jax.ShapeDtypeStruct((s, d), BF16),
        compiler_params=_params("parallel"), name="rmsnorm")(x, g.reshape(1, d))


def _residual_norm(x, y, g_post, g_next):
    s, d = x.shape
    bm = _pick(s, (256, 128, 8))
    row = pl.BlockSpec((bm, d), lambda i: (i, 0))
    vec = pl.BlockSpec((1, d), lambda i: (0, 0))
    if g_next is None:
        return pl.pallas_call(
            _res_kernel, grid=(s // bm,), in_specs=[row, row, vec], out_specs=row,
            out_shape=jax.ShapeDtypeStruct((s, d), F32),
            compiler_params=_params("parallel"), name="residual")(x, y, g_post.reshape(1, d)), None
    return pl.pallas_call(
        _resnorm_kernel, grid=(s // bm,), in_specs=[row, row, vec, vec], out_specs=[row, row],
        out_shape=[jax.ShapeDtypeStruct((s, d), F32), jax.ShapeDtypeStruct((s, d), BF16)],
        compiler_params=_params("parallel"), name="residual_norm")(
            x, y, g_post.reshape(1, d), g_next.reshape(1, d))


def _mm_kernel(a_ref, w_ref, o_ref, *, sigmoid):
    acc = jnp.dot(a_ref[...], w_ref[...], preferred_element_type=F32)
    if sigmoid:
        acc = jax.nn.sigmoid(acc)
    o_ref[...] = acc.astype(o_ref.dtype)


def _mm_colscale_kernel(a_ref, w_ref, s_ref, o_ref):
    acc = jnp.dot(a_ref[...], w_ref[...], preferred_element_type=F32)
    o_ref[...] = (acc * s_ref[...]).astype(o_ref.dtype)


def _cast_kernel(w_ref, o_ref):
    o_ref[...] = w_ref[...].astype(o_ref.dtype)


def _to_bf16(w):
    depth, k, n = w.shape
    bk = _pick(k, [c for c in (512, 256, 128, 64, 32, 16) if c * n * 4 <= CAST_BLOCK_BYTES])
    spec = pl.BlockSpec((1, bk, n), lambda l, i: (l, i, 0))
    return pl.pallas_call(
        _cast_kernel, grid=(depth, k // bk), in_specs=[spec], out_specs=spec,
        out_shape=jax.ShapeDtypeStruct(w.shape, BF16),
        compiler_params=_params("parallel", "parallel"), name="weight_cast")(w)


def _matmul_colscale(a, w, layer, col_scale, out_dtype, *, bm, bn, name):
    m, k = a.shape
    n = w.shape[2]
    return pl.pallas_call(
        _mm_colscale_kernel,
        grid=(m // bm, n // bn),
        in_specs=[pl.BlockSpec((bm, k), lambda i, j: (i, 0)),
                  pl.BlockSpec((None, k, bn), lambda i, j: (layer, 0, j)),
                  pl.BlockSpec((1, bn), lambda i, j: (0, j))],
        out_specs=pl.BlockSpec((bm, bn), lambda i, j: (i, j)),
        out_shape=jax.ShapeDtypeStruct((m, n), out_dtype),
        compiler_params=_params("parallel", "arbitrary"), name=name)(a, w, col_scale.reshape(1, n))


def _matmul(a, w, layer, out_dtype, *, bm, bn, sigmoid=False, name):
    m, k = a.shape
    n = w.shape[2]
    return pl.pallas_call(
        functools.partial(_mm_kernel, sigmoid=sigmoid),
        grid=(m // bm, n // bn),
        in_specs=[pl.BlockSpec((bm, k), lambda i, j: (i, 0)),
                  pl.BlockSpec((None, k, bn), lambda i, j: (layer, 0, j))],
        out_specs=pl.BlockSpec((bm, bn), lambda i, j: (i, j)),
        out_shape=jax.ShapeDtypeStruct((m, n), out_dtype),
        compiler_params=_params("parallel", "arbitrary"), name=name)(a, w)


def _merge_kernel(ya_ref, yb_ref, yc_ref, wa_ref, wb_ref, wc_ref, ga_ref, gb_ref, gc_ref, o_ref):
    pa = jnp.dot(ya_ref[...], wa_ref[...], preferred_element_type=F32)
    pb = jnp.dot(yb_ref[...], wb_ref[...], preferred_element_type=F32)
    pc = jnp.dot(yc_ref[...], wc_ref[...], preferred_element_type=F32)
    merged = (ga_ref[...].astype(F32) * pa + gb_ref[...].astype(F32) * pb
              + gc_ref[...].astype(F32) * pc)
    o_ref[...] = merged.astype(o_ref.dtype)


def _merge(ya, yb, yc, w, layer, gates, *, bm, bn):
    s = ya.shape[0]
    d = w.shape[2]
    nj = d // bn
    ka, kb, kc = ya.shape[1], yb.shape[1], yc.shape[1]
    assert ka % kb == 0 and (ka + kb) % kc == 0, "branch widths must tile the rows of w_branch"

    def act(k):
        return pl.BlockSpec((bm, k), lambda i, j: (i, 0))

    def wgt(k, row_off):
        return pl.BlockSpec((None, k, bn), lambda i, j: (layer, row_off // k, j))

    def gate(b):
        return pl.BlockSpec((bm, bn), lambda i, j: (i, b * nj + j))

    return pl.pallas_call(
        _merge_kernel, grid=(s // bm, nj),
        in_specs=[act(ka), act(kb), act(kc), wgt(ka, 0), wgt(kb, ka), wgt(kc, ka + kb),
                  gate(0), gate(1), gate(2)],
        out_specs=pl.BlockSpec((bm, bn), lambda i, j: (i, j)),
        out_shape=jax.ShapeDtypeStruct((s, d), BF16),
        compiler_params=_params("parallel", "arbitrary"), name="branch_merge")(
            ya, yb, yc, w, w, w, gates, gates, gates)


def _ffn_in_kernel(h_ref, wg_ref, wu_ref, o_ref):
    h = h_ref[...]
    g = jnp.dot(h, wg_ref[...], preferred_element_type=F32)
    u = jnp.dot(h, wu_ref[...], preferred_element_type=F32)
    o_ref[...] = (g * jax.nn.sigmoid(g) * u).astype(o_ref.dtype)


def _ffn_in(h, w, layer, *, bm, bn):
    s, d = h.shape
    d_ff = w.shape[2] // 2
    nj = d_ff // bn
    return pl.pallas_call(
        _ffn_in_kernel, grid=(s // bm, nj),
        in_specs=[pl.BlockSpec((bm, d), lambda i, j: (i, 0)),
                  pl.BlockSpec((None, d, bn), lambda i, j: (layer, 0, j)),
                  pl.BlockSpec((None, d, bn), lambda i, j: (layer, 0, nj + j))],
        out_specs=pl.BlockSpec((bm, bn), lambda i, j: (i, j)),
        out_shape=jax.ShapeDtypeStruct((s, d_ff), BF16),
        compiler_params=_params("parallel", "arbitrary"), name="ffn_in")(h, w, w)


def _causal_conv(x, w_ref, xbuf, width):
    t = x.shape[0]
    xbuf[CONV_HALO:CONV_HALO + t, :] = x
    out = None
    for k in range(width):
        off = CONV_HALO - (width - 1) + k
        term = w_ref[k:k + 1, :] * xbuf[off:off + t, :]
        out = term if out is None else out + term
    xbuf[0:CONV_HALO, :] = xbuf[t:t + CONV_HALO, :]
    return out


def _linear_scan(a, u, h0):
    t = a.shape[0]
    sub = F32_SUBLANES
    pos = lax.broadcasted_iota(jnp.int32, a.shape, 0) & (sub - 1)
    s = 1
    while s < sub:
        keep = pos >= s
        a_prev = pltpu.roll(a, s, 0)
        u_prev = pltpu.roll(u, s, 0)
        u = jnp.where(keep, a * u_prev + u, u)
        a = jnp.where(keep, a * a_prev, a)
        s *= 2
    groups = []
    carry = h0
    for g in range(t // sub):
        hg = u[g * sub:(g + 1) * sub] + a[g * sub:(g + 1) * sub] * carry
        groups.append(hg)
        carry = hg[sub - 1:sub]
    return jnp.concatenate(groups, axis=0)


def _sigmoid(x):
    return 0.5 * jnp.tanh(0.5 * x) + 0.5


def _mix_a_kernel(xa_ref, ga_ref, cw_ref, cb_ref, wa_ref, ba_ref, wx_ref, bx_ref, lam_ref,
                  o_ref, xbuf, hcar):
    @pl.when(pl.program_id(1) == 0)
    def _():
        xbuf[0:CONV_HALO, :] = jnp.zeros((CONV_HALO, xbuf.shape[1]), F32)
        hcar[...] = jnp.zeros_like(hcar)

    t, c = xa_ref.shape
    x = _causal_conv(xa_ref[...].astype(F32), cw_ref, xbuf, cw_ref.shape[0]) + cb_ref[...]
    xb = x.astype(BF16)
    r_parts, i_parts = [], []
    for g in range(c // LRU_BLOCK_DIM):
        xg = xb[:, g * LRU_BLOCK_DIM:(g + 1) * LRU_BLOCK_DIM]
        r_parts.append(jnp.dot(xg, wa_ref[g], preferred_element_type=F32))
        i_parts.append(jnp.dot(xg, wx_ref[g], preferred_element_type=F32))
    r = _sigmoid(jnp.concatenate(r_parts, axis=1) + ba_ref[...])
    i = _sigmoid(jnp.concatenate(i_parts, axis=1) + bx_ref[...])
    z = -lam_ref[...]
    softplus = jnp.maximum(z, 0.0) + jnp.log1p(jnp.exp(-jnp.abs(z)))
    log_a = (-LRU_C) * r * softplus
    a = jnp.exp(log_a)
    th = jnp.tanh(log_a)
    w = -2.0 * th
    mult = w * lax.rsqrt(jnp.maximum(w, F32_TINY)) * lax.rsqrt(1.0 - th)
    u = mult * (i * x)
    h = _linear_scan(a, u, hcar[...])
    hcar[...] = h[t - 1:t, :]
    o_ref[...] = (jax.nn.gelu(ga_ref[...].astype(F32)) * h).astype(o_ref.dtype)


def _mix_a(proj, col_x, col_g, conv_w, conv_b, w_a, b_a, w_x, b_x, lam):
    s = proj.shape[0]
    width = conv_w.shape[1]
    tc = CHAN_TILE
    tt = _pick(s, (TIME_TILE,))
    gpt = tc // LRU_BLOCK_DIM
    x_off, g_off = col_x // tc, col_g // tc
    vec = pl.BlockSpec((1, tc), lambda c, t: (0, c))
    blk = pl.BlockSpec((gpt, LRU_BLOCK_DIM, LRU_BLOCK_DIM), lambda c, t: (c, 0, 0))
    return pl.pallas_call(
        _mix_a_kernel, grid=(width // tc, s // tt),
        in_specs=[pl.BlockSpec((tt, tc), lambda c, t: (t, x_off + c)),
                  pl.BlockSpec((tt, tc), lambda c, t: (t, g_off + c)),
                  pl.BlockSpec((conv_w.shape[0], tc), lambda c, t: (0, c)),
                  vec, blk, vec, blk, vec, vec],
        out_specs=pl.BlockSpec((tt, tc), lambda c, t: (t, c)),
        out_shape=jax.ShapeDtypeStruct((s, width), BF16),
        scratch_shapes=[pltpu.VMEM((tt + CONV_HALO, tc), F32), pltpu.VMEM((1, tc), F32)],
        compiler_params=_params("arbitrary", "arbitrary"), name="mixer_rglru")(
            proj, proj, conv_w, conv_b.reshape(1, width), w_a.astype(BF16), b_a.reshape(1, width),
            w_x.astype(BF16), b_x.reshape(1, width), lam.reshape(1, width))


def _mix_c_kernel(sb_ref, sc_ref, sx_ref, cw_ref, o_ref, xbuf):
    @pl.when(pl.program_id(1) == 0)
    def _():
        xbuf[0:CONV_HALO, :] = jnp.zeros((CONV_HALO, xbuf.shape[1]), F32)

    z = sc_ref[...].astype(F32) * sx_ref[...].astype(F32)
    conv = _causal_conv(z, cw_ref, xbuf, cw_ref.shape[0])
    o_ref[...] = (sb_ref[...].astype(F32) * conv).astype(o_ref.dtype)


def _mix_c(proj, col_b, col_c, col_x, conv_w):
    s = proj.shape[0]
    width = conv_w.shape[1]
    tc = CHAN_TILE
    tt = _pick(s, (TIME_TILE,))

    def col(off):
        return pl.BlockSpec((tt, tc), lambda c, t: (t, off // tc + c))

    return pl.pallas_call(
        _mix_c_kernel, grid=(width // tc, s // tt),
        in_specs=[col(col_b), col(col_c), col(col_x),
                  pl.BlockSpec((conv_w.shape[0], tc), lambda c, t: (0, c))],
        out_specs=pl.BlockSpec((tt, tc), lambda c, t: (t, c)),
        out_shape=jax.ShapeDtypeStruct((s, width), BF16),
        scratch_shapes=[pltpu.VMEM((tt + CONV_HALO, tc), F32)],
        compiler_params=_params("arbitrary", "arbitrary"), name="mixer_shortconv")(
            proj, proj, proj, conv_w)


def _t5_bucket_np(dist):
    n = np.maximum(dist, 0)
    max_exact = REL_BUCKETS // 2
    nf = np.maximum(n, 1).astype(np.float32)
    large = max_exact + (np.log(nf / np.float32(max_exact)) / np.float32(math.log(REL_MAX_DIST / max_exact))
                         * np.float32(REL_BUCKETS - max_exact)).astype(np.int32)
    large = np.minimum(large, REL_BUCKETS - 1)
    return np.where(n < max_exact, n, large).astype(np.int32)


def _near_offsets():
    d = 0
    while True:
        lo = d * MOBA_BLOCK - (MOBA_BLOCK - 1)
        if lo > 0 and _t5_bucket_np(np.array([lo]))[0] == REL_BUCKETS - 1:
            return d
        d += 1


N_NEAR = _near_offsets()


KEY_GROUP = 4


def _strip_deltas(group):
    return list(range(N_NEAR - 1 + 2 * group - 1, -group, -1))


def _bucket_strip(group):
    kr = np.arange(MOBA_BLOCK)[:, None]
    qc = np.arange(MOBA_BLOCK)[None, :]
    tiles = []
    for d in _strip_deltas(group):
        rel = d * MOBA_BLOCK + qc - kr
        tiles.append(np.where(rel >= 0, _t5_bucket_np(rel), REL_BUCKETS))
    return np.stack(tiles).astype(np.int32)


def _bias_kernel(tbl_ref, bucket_ref, o_ref):
    h = pl.program_id(0)
    bucket = bucket_ref[0]
    acc = jnp.full(bucket.shape, NEG, F32)
    for b in range(REL_BUCKETS):
        acc = jnp.where(bucket == b, tbl_ref[h * REL_BUCKETS + b], acc)
    o_ref[0] = acc


def _bias_strip(rel_bias, group):
    n_heads = rel_bias.shape[1]
    tbl = (rel_bias.astype(F32).T * LOG2E).reshape(-1)
    buckets = jnp.asarray(_bucket_strip(group))
    n_strip = buckets.shape[0]
    blk = MOBA_BLOCK
    return pl.pallas_call(
        _bias_kernel, grid=(n_heads, n_strip),
        in_specs=[pl.BlockSpec(memory_space=pltpu.SMEM),
                  pl.BlockSpec((1, blk, blk), lambda h, d: (d, 0, 0))],
        out_specs=pl.BlockSpec((1, blk, blk), lambda h, d: (h, d, 0)),
        out_shape=jax.ShapeDtypeStruct((n_heads, n_strip * blk, blk), F32),
        compiler_params=_params("arbitrary", "arbitrary"), name="t5_bias_strip")(tbl, buckets)


def _kmean_kernel(k_ref, o_ref):
    k = k_ref[...].astype(F32)
    nb = k.shape[0] // MOBA_BLOCK
    o_ref[0] = jnp.mean(k.reshape(nb, MOBA_BLOCK, k.shape[1]), axis=1)


def _kmean(proj, col_k, n_heads):
    s = proj.shape[0]
    nb = s // MOBA_BLOCK
    off = col_k // HEAD_DIM
    return pl.pallas_call(
        _kmean_kernel, grid=(n_heads,),
        in_specs=[pl.BlockSpec((s, HEAD_DIM), lambda h: (0, off + h))],
        out_specs=pl.BlockSpec((1, nb, HEAD_DIM), lambda h: (h, 0, 0)),
        out_shape=jax.ShapeDtypeStruct((n_heads, nb, HEAD_DIM), F32),
        compiler_params=_params("parallel"), name="moba_kmean")(proj)


_NT = (((1,), (1,)), ((), ()))
_TN = (((0,), (0,)), ((), ()))


def _attn_kernel(q_ref, k_ref, v_ref, km_ref, bias_ref, o_ref, neg_ref, acc_ref, *, group):
    i = pl.program_id(1)
    blk = MOBA_BLOCK
    gk = group * blk
    nb = neg_ref.shape[0]
    q = q_ref[...]

    km = km_ref[0]
    km_hi = km.astype(BF16)
    km_lo = (km - km_hi.astype(F32)).astype(BF16)
    gate = (lax.dot_general(km_hi, q, _NT, preferred_element_type=F32)
            + lax.dot_general(km_lo, q, _NT, preferred_element_type=F32))
    bidx = lax.broadcasted_iota(jnp.int32, (nb, blk), 0).astype(F32)
    own = i.astype(F32)
    past = bidx < own
    g = jnp.where(past, gate, NEG)
    chosen = jnp.zeros((nb, blk), jnp.bool_)
    for _ in range(MOBA_TOPK):
        mx = jnp.max(g, axis=0, keepdims=True)
        first = jnp.min(jnp.where(g == mx, bidx, float(nb)), axis=0, keepdims=True)
        pick = bidx == first
        chosen = jnp.logical_or(chosen, pick)
        g = jnp.where(pick, -jnp.inf, g)
    attend = jnp.logical_or(jnp.logical_and(chosen, past), bidx == own)
    neg_ref[...] = jnp.where(attend, 0.0, NEG)
    acc_ref[...] = jnp.zeros_like(acc_ref)

    def scores(gi):
        hk = gk // 2
        halves = []
        for part in range(2):
            kb = k_ref[pl.ds(pl.multiple_of(gi * gk + part * hk, hk), hk), :]
            halves.append(lax.dot_general(kb, q, _NT, preferred_element_type=F32))
        return jnp.concatenate(halves, axis=0)

    def update(s, gi, m, acc):
        top = jnp.maximum(_strip_deltas(group)[0] - (i - gi * group), 0)
        s = s + bias_ref[0, pl.ds(pl.multiple_of(top * blk, blk), gk), :]
        pieces, rows = [], []
        m_new = m
        for b in range(group):
            row = neg_ref[pl.ds(gi * group + b, 1), :]
            piece = s[b * blk:(b + 1) * blk]
            m_new = jnp.maximum(m_new, jnp.max(piece, axis=0, keepdims=True) + row)
            pieces.append(piece)
            rows.append(row)
        probs = [jnp.exp2((piece - (m_new - row)).astype(BF16)) for piece, row in zip(pieces, rows)]
        p = jnp.concatenate(probs, axis=0)
        alpha = jnp.exp2(m - m_new)
        hk = gk // 2
        ones = jnp.ones((hk, BF16_SUBLANES), BF16)
        pv, den = None, None
        for part in range(2):
            vb = v_ref[pl.ds(pl.multiple_of(gi * gk + part * hk, hk), hk), :]
            ph = p[part * hk:(part + 1) * hk]
            pv_h = lax.dot_general(vb, ph, _TN, preferred_element_type=F32)
            den_h = lax.dot_general(ones, ph, _TN, preferred_element_type=F32)
            pv = pv_h if pv is None else pv + pv_h
            den = den_h if den is None else den + den_h
        dh = vb.shape[1]
        acc[0:dh, :] = acc[0:dh, :] * alpha + pv
        acc[dh:, :] = acc[dh:, :] * alpha + den
        return m_new

    acc_a, acc_b = acc_ref.at[0], acc_ref.at[1]
    g_own = i // group
    m0 = update(scores(g_own), g_own, jnp.full((1, blk), NEG, F32), acc_a)

    def pair(t, carry):
        m_a, m_b = carry
        g_a = g_own - 1 - 2 * t
        g_b = g_a - 1
        s_a = scores(g_a)
        s_b = scores(g_b)
        m_a = update(s_a, g_a, m_a, acc_a)
        m_b = update(s_b, g_b, m_b, acc_b)
        return m_a, m_b

    m_a, m_b = lax.fori_loop(0, g_own // 2, pair, (m0, m0))
    m_a = lax.cond(g_own % 2 == 1, lambda m: update(scores(0), 0, m, acc_a), lambda m: m, m_a)
    m = jnp.maximum(m_a, m_b)
    merged = acc_a[...] * jnp.exp2(m_a - m) + acc_b[...] * jnp.exp2(m_b - m)
    dh = o_ref.shape[1]
    out = merged[0:dh] * (1.0 / merged[dh:dh + 1])
    o_ref[...] = out.T.astype(o_ref.dtype)


def _key_group(seq):
    nb = seq // MOBA_BLOCK
    return KEY_GROUP if nb % KEY_GROUP == 0 else 1


def _moba(proj, col_q, col_k, col_v, n_heads, bias):
    s = proj.shape[0]
    blk, dh = MOBA_BLOCK, HEAD_DIM
    nb = s // blk
    group = _key_group(s)
    km = _kmean(proj, col_k, n_heads)
    vrows = dh + BF16_SUBLANES
    q_off, k_off, v_off = col_q // dh, col_k // dh, col_v // dh
    return pl.pallas_call(
        functools.partial(_attn_kernel, group=group), grid=(n_heads, nb),
        in_specs=[pl.BlockSpec((blk, dh), lambda h, i: (i, q_off + h)),
                  pl.BlockSpec((s, dh), lambda h, i: (0, k_off + h)),
                  pl.BlockSpec((s, dh), lambda h, i: (0, v_off + h)),
                  pl.BlockSpec((1, nb, dh), lambda h, i: (h, 0, 0)),
                  pl.BlockSpec((1, bias.shape[1], blk), lambda h, i: (h, 0, 0))],
        out_specs=pl.BlockSpec((blk, dh), lambda h, i: (i, h)),
        out_shape=jax.ShapeDtypeStruct((s, n_heads * dh), BF16),
        scratch_shapes=[pltpu.VMEM((nb, blk), F32), pltpu.VMEM((2, vrows, blk), F32)],
        compiler_params=_params("parallel", "arbitrary"), name="moba_attention")(
            proj, proj, proj, km, bias)


def _layer(x, h, layer, bias_strip, g_post_mix, g_pre_ffn, g_post_ffn, g_next, w_in, conv_a_w, conv_a_b,
           lru_wa, lru_ba, lru_wx, lru_bx, lru_lambda, conv_c_w, w_branch, w_gate, w_out,
           w_ffn_in, w_ffn_out):
    s, d = x.shape
    in_w = w_in.shape[2]
    lru_w = conv_a_w.shape[1]
    sc_w = conv_c_w.shape[1]
    att_w = (in_w - 2 * lru_w - 3 * sc_w) // 3
    n_heads = att_w // HEAD_DIM
    col_ga = lru_w
    col_q = 2 * lru_w
    col_k = col_q + att_w
    col_v = col_k + att_w
    col_sb = col_v + att_w
    col_sc = col_sb + sc_w
    col_sx = col_sc + sc_w

    bm = _pick(s, (1024, 512, 256))
    col = jnp.arange(in_w)
    col_scale = jnp.where((col >= col_q) & (col < col_k), QK_SCALE, 1.0).astype(F32)
    proj = _matmul_colscale(h, w_in, layer, col_scale, BF16, bm=bm,
                            bn=_pick(in_w, (768, 512, 256, 128)), name="in_proj")
    gates = _matmul(h, w_gate, layer, BF16, bm=bm, bn=_pick(w_gate.shape[2], (1024, 512, 256, 128)),
                    sigmoid=True, name="gate_proj")

    ya = _mix_a(proj, 0, col_ga, conv_a_w, conv_a_b, lru_wa, lru_ba, lru_wx, lru_bx, lru_lambda)
    yb = _moba(proj, col_q, col_k, col_v, n_heads, bias_strip)
    yc = _mix_c(proj, col_sb, col_sc, col_sx, conv_c_w)

    merged = _merge(ya, yb, yc, w_branch, layer, gates, bm=bm, bn=_pick(d, (512, 256, 128)))
    mix = _matmul(merged, w_out, layer, BF16, bm=bm, bn=_pick(d, (512, 256, 128)), name="out_proj")
    x, h2 = _residual_norm(x, mix, g_post_mix, g_pre_ffn)

    act = _ffn_in(h2, w_ffn_in, layer, bm=_pick(s, (2048, 1024, 512, 256)),
                  bn=_pick(w_ffn_in.shape[2] // 2, (256, 128)))
    f = _matmul(act, w_ffn_out, layer, BF16, bm=_pick(s, (512, 256)), bn=_pick(d, (512, 256, 128)),
                name="ffn_out")
    return _residual_norm(x, f, g_post_ffn, g_next)


@jax.jit
def _forward(x, rel_bias, norm_pre_mix, norm_post_mix, norm_pre_ffn, norm_post_ffn, w_in, conv_a_w,
             conv_a_b, lru_wa, lru_ba, lru_wx, lru_bx, lru_lambda, conv_c_w, w_branch, w_gate, w_out,
             w_ffn_in, w_ffn_out):
    bsz, s, d = x.shape
    depth = w_in.shape[0]
    bias_strip = _bias_strip(rel_bias, _key_group(s))
    w_in, w_branch, w_gate, w_out, w_ffn_in, w_ffn_out = (
        _to_bf16(w) for w in (w_in, w_branch, w_gate, w_out, w_ffn_in, w_ffn_out))
    outs = []
    for b in range(bsz):
        xb = x.reshape(s, d) if bsz == 1 else x[b]
        h = _norm(xb, norm_pre_mix[0])
        for l in range(depth):
            g_next = norm_pre_mix[l + 1] if l + 1 < depth else None
            xb, h = _layer(xb, h, l, bias_strip, norm_post_mix[l], norm_pre_ffn[l], norm_post_ffn[l], g_next,
                           w_in, conv_a_w[l], conv_a_b[l], lru_wa[l], lru_ba[l], lru_wx[l], lru_bx[l],
                           lru_lambda[l], conv_c_w[l], w_branch, w_gate, w_out, w_ffn_in, w_ffn_out)
        outs.append(xb)
    return outs[0].reshape(1, s, d) if bsz == 1 else jnp.stack(outs)


def kernel(x, rel_bias, norm_pre_mix, norm_post_mix, norm_pre_ffn, norm_post_ffn, w_in, conv_a_w, conv_a_b, lru_wa, lru_ba, lru_wx, lru_bx, lru_lambda, conv_c_w, w_branch, w_gate, w_out, w_ffn_in, w_ffn_out):
    return _forward(x, rel_bias, norm_pre_mix, norm_post_mix, norm_pre_ffn, norm_post_ffn, w_in, conv_a_w,
                    conv_a_b, lru_wa, lru_ba, lru_wx, lru_bx, lru_lambda, conv_c_w, w_branch, w_gate, w_out,
                    w_ffn_in, w_ffn_out)
```

```python
import functools
import math

import numpy as np
import jax
import jax.numpy as jnp
from jax import lax
from jax.experimental import pallas as pl
from jax.experimental.pallas import tpu as pltpu

F32 = jnp.float32
BF16 = jnp.bfloat16

EPS = 1e-6
NEG = -1e30
LOG2E = 1.4426950408889634

HEAD_DIM = 128
QK_SCALE = HEAD_DIM ** -0.5 * LOG2E
MOBA_BLOCK = 256
MOBA_TOPK = 3
REL_BUCKETS = 32
REL_MAX_DIST = 2048
LRU_BLOCK_DIM = 128
LRU_C = 8.0
N_BRANCH = 3
CHAN_TILE = 512
TIME_TILE = 512
CONV_HALO = 8
BF16_SUBLANES = 16
F32_SUBLANES = 8
F32_TINY = float(np.finfo(np.float32).tiny)
VMEM_LIMIT = 56 * 1024 * 1024
CAST_BLOCK_BYTES = 4 * 1024 * 1024


def _pick(n, candidates):
    for c in candidates:
        if n % c == 0:
            return c
    raise ValueError(f"no tile in {candidates} divides {n}")


def _params(*sem):
    return pltpu.CompilerParams(dimension_semantics=sem, vmem_limit_bytes=VMEM_LIMIT)


def _rms(x, g):
    return x * lax.rsqrt(jnp.mean(x * x, axis=-1, keepdims=True) + EPS) * g


def _norm_kernel(x_ref, g_ref, h_ref):
    h_ref[...] = _rms(x_ref[...], g_ref[...]).astype(h_ref.dtype)


def _resnorm_kernel(x_ref, y_ref, gp_ref, gn_ref, xo_ref, h_ref):
    xn = x_ref[...] + _rms(y_ref[...].astype(F32), gp_ref[...])
    xo_ref[...] = xn
    h_ref[...] = _rms(xn, gn_ref[...]).astype(h_ref.dtype)


def _res_kernel(x_ref, y_ref, gp_ref, xo_ref):
    xo_ref[...] = x_ref[...] + _rms(y_ref[...].astype(F32), gp_ref[...])


def _norm(x, g):
    s, d = x.shape
    bm = _pick(s, (256, 128, 8))
    row = pl.BlockSpec((bm, d), lambda i: (i, 0))
    vec = pl.BlockSpec((1, d), lambda i: (0, 0))
    return pl.pallas_call(
        _norm_kernel, grid=(s // bm,), in_specs=[row, vec], out_specs=row,
        out_shape=jax.ShapeDtypeStruct((s, d), BF16),
        compiler_params=_params("parallel"), name="rmsnorm")(x, g.reshape(1, d))


def _residual_norm(x, y, g_post, g_next):
    s, d = x.shape
    bm = _pick(s, (256, 128, 8))
    row = pl.BlockSpec((bm, d), lambda i: (i, 0))
    vec = pl.BlockSpec((1, d), lambda i: (0, 0))
    if g_next is None:
        return pl.pallas_call(
            _res_kernel, grid=(s // bm,), in_specs=[row, row, vec], out_specs=row,
            out_shape=jax.ShapeDtypeStruct((s, d), F32),
            compiler_params=_params("parallel"), name="residual")(x, y, g_post.reshape(1, d)), None
    return pl.pallas_call(
        _resnorm_kernel, grid=(s // bm,), in_specs=[row, row, vec, vec], out_specs=[row, row],
        out_shape=[jax.ShapeDtypeStruct((s, d), F32), jax.ShapeDtypeStruct((s, d), BF16)],
        compiler_params=_params("parallel"), name="residual_norm")(
            x, y, g_post.reshape(1, d), g_next.reshape(1, d))


def _mm_kernel(a_ref, w_ref, o_ref, *, sigmoid):
    acc = jnp.dot(a_ref[...], w_ref[...], preferred_element_type=F32)
    if sigmoid:
        acc = jax.nn.sigmoid(acc)
    o_ref[...] = acc.astype(o_ref.dtype)


def _mm_colscale_kernel(a_ref, w_ref, s_ref, o_ref):
    acc = jnp.dot(a_ref[...], w_ref[...], preferred_element_type=F32)
    o_ref[...] = (acc * s_ref[...]).astype(o_ref.dtype)


def _cast_kernel(w_ref, o_ref):
    o_ref[...] = w_ref[...].astype(o_ref.dtype)


def _to_bf16(w):
    depth, k, n = w.shape
    bk = _pick(k, [c for c in (512, 256, 128, 64, 32, 16) if c * n * 4 <= CAST_BLOCK_BYTES])
    spec = pl.BlockSpec((1, bk, n), lambda l, i: (l, i, 0))
    return pl.pallas_call(
        _cast_kernel, grid=(depth, k // bk), in_specs=[spec], out_specs=spec,
        out_shape=jax.ShapeDtypeStruct(w.shape, BF16),
        compiler_params=_params("parallel", "parallel"), name="weight_cast")(w)


def _matmul_colscale(a, w, layer, col_scale, out_dtype, *, bm, bn, name):
    m, k = a.shape
    n = w.shape[2]
    return pl.pallas_call(
        _mm_colscale_kernel,
        grid=(m // bm, n // bn),
        in_specs=[pl.BlockSpec((bm, k), lambda i, j: (i, 0)),
                  pl.BlockSpec((None, k, bn), lambda i, j: (layer, 0, j)),
                  pl.BlockSpec((1, bn), lambda i, j: (0, j))],
        out_specs=pl.BlockSpec((bm, bn), lambda i, j: (i, j)),
        out_shape=jax.ShapeDtypeStruct((m, n), out_dtype),
        compiler_params=_params("parallel", "arbitrary"), name=name)(a, w, col_scale.reshape(1, n))


def _matmul(a, w, layer, out_dtype, *, bm, bn, sigmoid=False, name):
    m, k = a.shape
    n = w.shape[2]
    return pl.pallas_call(
        functools.partial(_mm_kernel, sigmoid=sigmoid),
        grid=(m // bm, n // bn),
        in_specs=[pl.BlockSpec((bm, k), lambda i, j: (i, 0)),
                  pl.BlockSpec((None, k, bn), lambda i, j: (layer, 0, j))],
        out_specs=pl.BlockSpec((bm, bn), lambda i, j: (i, j)),
        out_shape=jax.ShapeDtypeStruct((m, n), out_dtype),
        compiler_params=_params("parallel", "arbitrary"), name=name)(a, w)


def _merge_kernel(ya_ref, yb_ref, yc_ref, wa_ref, wb_ref, wc_ref, ga_ref, gb_ref, gc_ref, o_ref):
    pa = jnp.dot(ya_ref[...], wa_ref[...], preferred_element_type=F32)
    pb = jnp.dot(yb_ref[...], wb_ref[...], preferred_element_type=F32)
    pc = jnp.dot(yc_ref[...], wc_ref[...], preferred_element_type=F32)
    merged = (ga_ref[...].astype(F32) * pa + gb_ref[...].astype(F32) * pb
              + gc_ref[...].astype(F32) * pc)
    o_ref[...] = merged.astype(o_ref.dtype)


def _merge(ya, yb, yc, w, layer, gates, *, bm, bn):
    s = ya.shape[0]
    d = w.shape[2]
    nj = d // bn
    ka, kb, kc = ya.shape[1], yb.shape[1], yc.shape[1]
    assert ka % kb == 0 and (ka + kb) % kc == 0, "branch widths must tile the rows of w_branch"

    def act(k):
        return pl.BlockSpec((bm, k), lambda i, j: (i, 0))

    def wgt(k, row_off):
        return pl.BlockSpec((None, k, bn), lambda i, j: (layer, row_off // k, j))

    def gate(b):
        return pl.BlockSpec((bm, bn), lambda i, j: (i, b * nj + j))

    return pl.pallas_call(
        _merge_kernel, grid=(s // bm, nj),
        in_specs=[act(ka), act(kb), act(kc), wgt(ka, 0), wgt(kb, ka), wgt(kc, ka + kb),
                  gate(0), gate(1), gate(2)],
        out_specs=pl.BlockSpec((bm, bn), lambda i, j: (i, j)),
        out_shape=jax.ShapeDtypeStruct((s, d), BF16),
        compiler_params=_params("parallel", "arbitrary"), name="branch_merge")(
            ya, yb, yc, w, w, w, gates, gates, gates)


def _ffn_in_kernel(h_ref, wg_ref, wu_ref, o_ref):
    h = h_ref[...]
    g = jnp.dot(h, wg_ref[...], preferred_element_type=F32)
    u = jnp.dot(h, wu_ref[...], preferred_element_type=F32)
    o_ref[...] = (g * jax.nn.sigmoid(g) * u).astype(o_ref.dtype)


def _ffn_in(h, w, layer, *, bm, bn):
    s, d = h.shape
    d_ff = w.shape[2] // 2
    nj = d_ff // bn
    return pl.pallas_call(
        _ffn_in_kernel, grid=(s // bm, nj),
        in_specs=[pl.BlockSpec((bm, d), lambda i, j: (i, 0)),
                  pl.BlockSpec((None, d, bn), lambda i, j: (layer, 0, j)),
                  pl.BlockSpec((None, d, bn), lambda i, j: (layer, 0, nj + j))],
        out_specs=pl.BlockSpec((bm, bn), lambda i, j: (i, j)),
        out_shape=jax.ShapeDtypeStruct((s, d_ff), BF16),
        compiler_params=_params("parallel", "arbitrary"), name="ffn_in")(h, w, w)


def _causal_conv(x, w_ref, xbuf, width):
    t = x.shape[0]
    xbuf[CONV_HALO:CONV_HALO + t, :] = x
    out = None
    for k in range(width):
        off = CONV_HALO - (width - 1) + k
        term = w_ref[k:k + 1, :] * xbuf[off:off + t, :]
        out = term if out is None else out + term
    xbuf[0:CONV_HALO, :] = xbuf[t:t + CONV_HALO, :]
    return out


def _linear_scan(a, u, h0):
    t = a.shape[0]
    sub = F32_SUBLANES
    pos = lax.broadcasted_iota(jnp.int32, a.shape, 0) & (sub - 1)
    s = 1
    while s < sub:
        keep = pos >= s
        a_prev = pltpu.roll(a, s, 0)
        u_prev = pltpu.roll(u, s, 0)
        u = jnp.where(keep, a * u_prev + u, u)
        a = jnp.where(keep, a * a_prev, a)
        s *= 2
    groups = []
    carry = h0
    for g in range(t // sub):
        hg = u[g * sub:(g + 1) * sub] + a[g * sub:(g + 1) * sub] * carry
        groups.append(hg)
        carry = hg[sub - 1:sub]
    return jnp.concatenate(groups, axis=0)


def _sigmoid(x):
    return 0.5 * jnp.tanh(0.5 * x) + 0.5


def _mix_a_kernel(xa_ref, ga_ref, cw_ref, cb_ref, wa_ref, ba_ref, wx_ref, bx_ref, lam_ref,
                  o_ref, xbuf, hcar):
    @pl.when(pl.program_id(1) == 0)
    def _():
        xbuf[0:CONV_HALO, :] = jnp.zeros((CONV_HALO, xbuf.shape[1]), F32)
        hcar[...] = jnp.zeros_like(hcar)

    t, c = xa_ref.shape
    x = _causal_conv(xa_ref[...].astype(F32), cw_ref, xbuf, cw_ref.shape[0]) + cb_ref[...]
    xb = x.astype(BF16)
    r_parts, i_parts = [], []
    for g in range(c // LRU_BLOCK_DIM):
        xg = xb[:, g * LRU_BLOCK_DIM:(g + 1) * LRU_BLOCK_DIM]
        r_parts.append(jnp.dot(xg, wa_ref[g], preferred_element_type=F32))
        i_parts.append(jnp.dot(xg, wx_ref[g], preferred_element_type=F32))
    r = _sigmoid(jnp.concatenate(r_parts, axis=1) + ba_ref[...])
    i = _sigmoid(jnp.concatenate(i_parts, axis=1) + bx_ref[...])
    z = -lam_ref[...]
    softplus = jnp.maximum(z, 0.0) + jnp.log1p(jnp.exp(-jnp.abs(z)))
    log_a = (-LRU_C) * r * softplus
    a = jnp.exp(log_a)
    th = jnp.tanh(log_a)
    w = -2.0 * th
    mult = w * lax.rsqrt(jnp.maximum(w, F32_TINY)) * lax.rsqrt(1.0 - th)
    u = mult * (i * x)
    h = _linear_scan(a, u, hcar[...])
    hcar[...] = h[t - 1:t, :]
    o_ref[...] = (jax.nn.gelu(ga_ref[...].astype(F32)) * h).astype(o_ref.dtype)


def _mix_a(proj, col_x, col_g, conv_w, conv_b, w_a, b_a, w_x, b_x, lam):
    s = proj.shape[0]
    width = conv_w.shape[1]
    tc = CHAN_TILE
    tt = _pick(s, (TIME_TILE,))
    gpt = tc // LRU_BLOCK_DIM
    x_off, g_off = col_x // tc, col_g // tc
    vec = pl.BlockSpec((1, tc), lambda c, t: (0, c))
    blk = pl.BlockSpec((gpt, LRU_BLOCK_DIM, LRU_BLOCK_DIM), lambda c, t: (c, 0, 0))
    return pl.pallas_call(
        _mix_a_kernel, grid=(width // tc, s // tt),
        in_specs=[pl.BlockSpec((tt, tc), lambda c, t: (t, x_off + c)),
                  pl.BlockSpec((tt, tc), lambda c, t: (t, g_off + c)),
                  pl.BlockSpec((conv_w.shape[0], tc), lambda c, t: (0, c)),
                  vec, blk, vec, blk, vec, vec],
        out_specs=pl.BlockSpec((tt, tc), lambda c, t: (t, c)),
        out_shape=jax.ShapeDtypeStruct((s, width), BF16),
        scratch_shapes=[pltpu.VMEM((tt + CONV_HALO, tc), F32), pltpu.VMEM((1, tc), F32)],
        compiler_params=_params("arbitrary", "arbitrary"), name="mixer_rglru")(
            proj, proj, conv_w, conv_b.reshape(1, width), w_a.astype(BF16), b_a.reshape(1, width),
            w_x.astype(BF16), b_x.reshape(1, width), lam.reshape(1, width))


def _mix_c_kernel(sb_ref, sc_ref, sx_ref, cw_ref, o_ref, xbuf):
    @pl.when(pl.program_id(1) == 0)
    def _():
        xbuf[0:CONV_HALO, :] = jnp.zeros((CONV_HALO, xbuf.shape[1]), F32)

    z = sc_ref[...].astype(F32) * sx_ref[...].astype(F32)
    conv = _causal_conv(z, cw_ref, xbuf, cw_ref.shape[0])
    o_ref[...] = (sb_ref[...].astype(F32) * conv).astype(o_ref.dtype)


def _mix_c(proj, col_b, col_c, col_x, conv_w):
    s = proj.shape[0]
    width = conv_w.shape[1]
    tc = CHAN_TILE
    tt = _pick(s, (TIME_TILE,))

    def col(off):
        return pl.BlockSpec((tt, tc), lambda c, t: (t, off // tc + c))

    return pl.pallas_call(
        _mix_c_kernel, grid=(width // tc, s // tt),
        in_specs=[col(col_b), col(col_c), col(col_x),
                  pl.BlockSpec((conv_w.shape[0], tc), lambda c, t: (0, c))],
        out_specs=pl.BlockSpec((tt, tc), lambda c, t: (t, c)),
        out_shape=jax.ShapeDtypeStruct((s, width), BF16),
        scratch_shapes=[pltpu.VMEM((tt + CONV_HALO, tc), F32)],
        compiler_params=_params("arbitrary", "arbitrary"), name="mixer_shortconv")(
            proj, proj, proj, conv_w)


def _t5_bucket_np(dist):
    n = np.maximum(dist, 0)
    max_exact = REL_BUCKETS // 2
    nf = np.maximum(n, 1).astype(np.float32)
    large = max_exact + (np.log(nf / np.float32(max_exact)) / np.float32(math.log(REL_MAX_DIST / max_exact))
                         * np.float32(REL_BUCKETS - max_exact)).astype(np.int32)
    large = np.minimum(large, REL_BUCKETS - 1)
    return np.where(n < max_exact, n, large).astype(np.int32)


def _near_offsets():
    d = 0
    while True:
        lo = d * MOBA_BLOCK - (MOBA_BLOCK - 1)
        if lo > 0 and _t5_bucket_np(np.array([lo]))[0] == REL_BUCKETS - 1:
            return d
        d += 1


N_NEAR = _near_offsets()


KEY_GROUP = 4
QK_SPLIT = 2


def _strip_deltas(group):
    return list(range(N_NEAR - 1 + 2 * group - 1, -group, -1))


def _bucket_strip(group):
    kr = np.arange(MOBA_BLOCK)[:, None]
    qc = np.arange(MOBA_BLOCK)[None, :]
    tiles = []
    for d in _strip_deltas(group):
        rel = d * MOBA_BLOCK + qc - kr
        tiles.append(np.where(rel >= 0, _t5_bucket_np(rel), REL_BUCKETS))
    return np.stack(tiles).astype(np.int32)


def _bias_kernel(tbl_ref, bucket_ref, o_ref):
    h = pl.program_id(0)
    n_strip, blk, _ = bucket_ref.shape

    def one_block(d, carry):
        bucket = bucket_ref[d]
        acc = jnp.full(bucket.shape, NEG, F32)
        for b in range(REL_BUCKETS):
            acc = jnp.where(bucket == b, tbl_ref[h * REL_BUCKETS + b], acc)
        o_ref[0, pl.ds(pl.multiple_of(d * blk, blk), blk), :] = acc
        return carry

    lax.fori_loop(0, n_strip, one_block, 0)


def _bias_strip(rel_bias, group):
    n_heads = rel_bias.shape[1]
    tbl = (rel_bias.astype(F32).T * LOG2E).reshape(-1)
    buckets = jnp.asarray(_bucket_strip(group))
    n_strip = buckets.shape[0]
    blk = MOBA_BLOCK
    return pl.pallas_call(
        _bias_kernel, grid=(n_heads,),
        in_specs=[pl.BlockSpec(memory_space=pltpu.SMEM),
                  pl.BlockSpec((n_strip, blk, blk), lambda h: (0, 0, 0))],
        out_specs=pl.BlockSpec((1, n_strip * blk, blk), lambda h: (h, 0, 0)),
        out_shape=jax.ShapeDtypeStruct((n_heads, n_strip * blk, blk), F32),
        compiler_params=_params("arbitrary"), name="t5_bias_strip")(tbl, buckets)


def _kmean_kernel(k_ref, o_ref):
    k = k_ref[...].astype(F32)
    nb = k.shape[0] // MOBA_BLOCK
    o_ref[0] = jnp.mean(k.reshape(nb, MOBA_BLOCK, k.shape[1]), axis=1)


def _kmean(proj, col_k, n_heads):
    s = proj.shape[0]
    nb = s // MOBA_BLOCK
    off = col_k // HEAD_DIM
    return pl.pallas_call(
        _kmean_kernel, grid=(n_heads,),
        in_specs=[pl.BlockSpec((s, HEAD_DIM), lambda h: (0, off + h))],
        out_specs=pl.BlockSpec((1, nb, HEAD_DIM), lambda h: (h, 0, 0)),
        out_shape=jax.ShapeDtypeStruct((n_heads, nb, HEAD_DIM), F32),
        compiler_params=_params("parallel"), name="moba_kmean")(proj)


_NT = (((1,), (1,)), ((), ()))
_TN = (((0,), (0,)), ((), ()))


def _attn_kernel(q_ref, k_ref, v_ref, km_ref, bias_ref, o_ref, neg_ref, acc_ref, *, group):
    i = pl.program_id(1)
    blk = MOBA_BLOCK
    gk = group * blk
    nb = neg_ref.shape[0]
    q = q_ref[...]

    km = km_ref[0]
    km_hi = km.astype(BF16)
    km_lo = (km - km_hi.astype(F32)).astype(BF16)
    gate = (lax.dot_general(km_hi, q, _NT, preferred_element_type=F32)
            + lax.dot_general(km_lo, q, _NT, preferred_element_type=F32))
    bidx = lax.broadcasted_iota(jnp.int32, (nb, blk), 0).astype(F32)
    own = i.astype(F32)
    past = bidx < own
    g = jnp.where(past, gate, NEG)
    chosen = jnp.zeros((nb, blk), jnp.bool_)
    for _ in range(MOBA_TOPK):
        mx = jnp.max(g, axis=0, keepdims=True)
        first = jnp.min(jnp.where(g == mx, bidx, float(nb)), axis=0, keepdims=True)
        pick = bidx == first
        chosen = jnp.logical_or(chosen, pick)
        g = jnp.where(pick, -jnp.inf, g)
    attend = jnp.logical_or(jnp.logical_and(chosen, past), bidx == own)
    neg_ref[...] = jnp.where(attend, 0.0, NEG)
    acc_ref[...] = jnp.zeros_like(acc_ref)

    def scores(gi):
        hk = gk // QK_SPLIT
        parts = []
        for part in range(QK_SPLIT):
            kb = k_ref[pl.ds(pl.multiple_of(gi * gk + part * hk, hk), hk), :]
            parts.append(lax.dot_general(kb, q, _NT, preferred_element_type=F32))
        return jnp.concatenate(parts, axis=0)

    def update(s, gi, m, acc):
        top = jnp.maximum(_strip_deltas(group)[0] - (i - gi * group), 0)
        s = s + bias_ref[0, pl.ds(pl.multiple_of(top * blk, blk), gk), :]
        pieces, rows = [], []
        m_new = m
        for b in range(group):
            row = neg_ref[pl.ds(gi * group + b, 1), :]
            piece = s[b * blk:(b + 1) * blk]
            m_new = jnp.maximum(m_new, jnp.max(piece, axis=0, keepdims=True) + row)
            pieces.append(piece)
            rows.append(row)
        probs = [jnp.exp2((piece - (m_new - row)).astype(BF16)) for piece, row in zip(pieces, rows)]
        p = jnp.concatenate(probs, axis=0)
        alpha = jnp.exp2(m - m_new)
        hk = gk // 2
        ones = jnp.ones((hk, BF16_SUBLANES), BF16)
        pv, den = None, None
        for part in range(2):
            vb = v_ref[pl.ds(pl.multiple_of(gi * gk + part * hk, hk), hk), :]
            ph = p[part * hk:(part + 1) * hk]
            pv_h = lax.dot_general(vb, ph, _TN, preferred_element_type=F32)
            den_h = lax.dot_general(ones, ph, _TN, preferred_element_type=F32)
            pv = pv_h if pv is None else pv + pv_h
            den = den_h if den is None else den + den_h
        dh = vb.shape[1]
        acc[0:dh, :] = acc[0:dh, :] * alpha + pv
        acc[dh:, :] = acc[dh:, :] * alpha + den
        return m_new

    acc_a, acc_b = acc_ref.at[0], acc_ref.at[1]
    g_own = i // group
    m0 = update(scores(g_own), g_own, jnp.full((1, blk), NEG, F32), acc_a)

    def pair(t, carry):
        m_a, m_b = carry
        g_a = g_own - 1 - 2 * t
        g_b = g_a - 1
        s_a = scores(g_a)
        s_b = scores(g_b)
        m_a = update(s_a, g_a, m_a, acc_a)
        m_b = update(s_b, g_b, m_b, acc_b)
        return m_a, m_b

    m_a, m_b = lax.fori_loop(0, g_own // 2, pair, (m0, m0))
    m_a = lax.cond(g_own % 2 == 1, lambda m: update(scores(0), 0, m, acc_a), lambda m: m, m_a)
    m = jnp.maximum(m_a, m_b)
    merged = acc_a[...] * jnp.exp2(m_a - m) + acc_b[...] * jnp.exp2(m_b - m)
    dh = o_ref.shape[1]
    out = merged[0:dh] * (1.0 / merged[dh:dh + 1])
    o_ref[...] = out.T.astype(o_ref.dtype)


def _key_group(seq):
    nb = seq // MOBA_BLOCK
    return KEY_GROUP if nb % KEY_GROUP == 0 else 1


def _moba(proj, col_q, col_k, col_v, n_heads, bias):
    s = proj.shape[0]
    blk, dh = MOBA_BLOCK, HEAD_DIM
    nb = s // blk
    group = _key_group(s)
    km = _kmean(proj, col_k, n_heads)
    vrows = dh + BF16_SUBLANES
    q_off, k_off, v_off = col_q // dh, col_k // dh, col_v // dh
    return pl.pallas_call(
        functools.partial(_attn_kernel, group=group), grid=(n_heads, nb),
        in_specs=[pl.BlockSpec((blk, dh), lambda h, i: (i, q_off + h)),
                  pl.BlockSpec((s, dh), lambda h, i: (0, k_off + h)),
                  pl.BlockSpec((s, dh), lambda h, i: (0, v_off + h)),
                  pl.BlockSpec((1, nb, dh), lambda h, i: (h, 0, 0)),
                  pl.BlockSpec((1, bias.shape[1], blk), lambda h, i: (h, 0, 0))],
        out_specs=pl.BlockSpec((blk, dh), lambda h, i: (i, h)),
        out_shape=jax.ShapeDtypeStruct((s, n_heads * dh), BF16),
        scratch_shapes=[pltpu.VMEM((nb, blk), F32), pltpu.VMEM((2, vrows, blk), F32)],
        compiler_params=_params("parallel", "arbitrary"), name="moba_attention")(
            proj, proj, proj, km, bias)


def _layer(x, h, layer, bias_strip, g_post_mix, g_pre_ffn, g_post_ffn, g_next, w_in, conv_a_w, conv_a_b,
           lru_wa, lru_ba, lru_wx, lru_bx, lru_lambda, conv_c_w, w_branch, w_gate, w_out,
           w_ffn_in, w_ffn_out):
    s, d = x.shape
    in_w = w_in.shape[2]
    lru_w = conv_a_w.shape[1]
    sc_w = conv_c_w.shape[1]
    att_w = (in_w - 2 * lru_w - 3 * sc_w) // 3
    n_heads = att_w // HEAD_DIM
    col_ga = lru_w
    col_q = 2 * lru_w
    col_k = col_q + att_w
    col_v = col_k + att_w
    col_sb = col_v + att_w
    col_sc = col_sb + sc_w
    col_sx = col_sc + sc_w

    bm = _pick(s, (1024, 512, 256))
    col = jnp.arange(in_w)
    col_scale = jnp.where((col >= col_q) & (col < col_k), QK_SCALE, 1.0).astype(F32)
    proj = _matmul_colscale(h, w_in, layer, col_scale, BF16, bm=bm,
                            bn=_pick(in_w, (768, 512, 256, 128)), name="in_proj")
    gates = _matmul(h, w_gate, layer, BF16, bm=bm, bn=_pick(w_gate.shape[2], (1024, 512, 256, 128)),
                    sigmoid=True, name="gate_proj")

    ya = _mix_a(proj, 0, col_ga, conv_a_w, conv_a_b, lru_wa, lru_ba, lru_wx, lru_bx, lru_lambda)
    yb = _moba(proj, col_q, col_k, col_v, n_heads, bias_strip)
    yc = _mix_c(proj, col_sb, col_sc, col_sx, conv_c_w)

    merged = _merge(ya, yb, yc, w_branch, layer, gates, bm=bm, bn=_pick(d, (512, 256, 128)))
    mix = _matmul(merged, w_out, layer, BF16, bm=bm, bn=_pick(d, (512, 256, 128)), name="out_proj")
    x, h2 = _residual_norm(x, mix, g_post_mix, g_pre_ffn)

    act = _ffn_in(h2, w_ffn_in, layer, bm=_pick(s, (2048, 1024, 512, 256)),
                  bn=_pick(w_ffn_in.shape[2] // 2, (256, 128)))
    f = _matmul(act, w_ffn_out, layer, BF16, bm=_pick(s, (512, 256)), bn=_pick(d, (512, 256, 128)),
                name="ffn_out")
    return _residual_norm(x, f, g_post_ffn, g_next)


@jax.jit
def _forward(x, rel_bias, norm_pre_mix, norm_post_mix, norm_pre_ffn, norm_post_ffn, w_in, conv_a_w,
             conv_a_b, lru_wa, lru_ba, lru_wx, lru_bx, lru_lambda, conv_c_w, w_branch, w_gate, w_out,
             w_ffn_in, w_ffn_out):
    bsz, s, d = x.shape
    depth = w_in.shape[0]
    bias_strip = _bias_strip(rel_bias, _key_group(s))
    w_in, w_branch, w_gate, w_out, w_ffn_in, w_ffn_out = (
        _to_bf16(w) for w in (w_in, w_branch, w_gate, w_out, w_ffn_in, w_ffn_out))
    outs = []
    for b in range(bsz):
        xb = x.reshape(s, d) if bsz == 1 else x[b]
        h = _norm(xb, norm_pre_mix[0])
        for l in range(depth):
            g_next = norm_pre_mix[l + 1] if l + 1 < depth else None
            xb, h = _layer(xb, h, l, bias_strip, norm_post_mix[l], norm_pre_ffn[l], norm_post_ffn[l], g_next,
                           w_in, conv_a_w[l], conv_a_b[l], lru_wa[l], lru_ba[l], lru_wx[l], lru_bx[l],
                           lru_lambda[l], conv_c_w[l], w_branch, w_gate, w_out, w_ffn_in, w_ffn_out)
        outs.append(xb)
    return outs[0].reshape(1, s, d) if bsz == 1 else jnp.stack(outs)


def kernel(x, rel_bias, norm_pre_mix, norm_post_mix, norm_pre_ffn, norm_post_ffn, w_in, conv_a_w, conv_a_b, lru_wa, lru_ba, lru_wx, lru_bx, lru_lambda, conv_c_w, w_branch, w_gate, w_out, w_ffn_in, w_ffn_out):
    return _forward(x, rel_bias, norm_pre_mix, norm_post_mix, norm_pre_ffn, norm_post_ffn, w_in, conv_a_w,
                    conv_a_b, lru_wa, lru_ba, lru_wx, lru_bx, lru_lambda, conv_c_w, w_branch, w_gate, w_out,
                    w_ffn_in, w_ffn_out)
```
